```python
import math
import jax, jax.numpy as jnp
from jax import lax
import numpy as np

D_MODEL = 1024
BATCH = 16
SEQ = 2048
DEPTH = 4

MIX_WIDTH = D_MODEL
GM_HEADS = 4
GM_HEAD_DIM = D_MODEL // 16
GM_WIDTH = GM_HEADS * GM_HEAD_DIM
GM_CHUNK = 128
MLA_HEADS = 8
MLA_NOPE = D_MODEL // 16
MLA_ROPE = D_MODEL // 32
MLA_V = D_MODEL // 16
MLA_WIDTH = MLA_HEADS * MLA_V
MLA_Q_RANK = D_MODEL // 4
MLA_KV_RANK = D_MODEL // 8
ROPE_THETA = 10000.0
ATTN_BLOCK = 128
MASK_VALUE = -1e30
HG_HEADS = 4
HG_KEY = D_MODEL // 16
HG_VAL = D_MODEL // 16
HG_KEY_WIDTH = HG_HEADS * HG_KEY
HG_WIDTH = HG_HEADS * HG_VAL
HG_CHUNK = 64
GATE_FLOOR = 1e-20
IN_WIDTHS = (GM_WIDTH, GM_WIDTH, MLA_Q_RANK, MLA_KV_RANK, MLA_ROPE,
             HG_KEY_WIDTH, HG_KEY_WIDTH, HG_WIDTH, HG_WIDTH)
IN_WIDTH = 2 * GM_WIDTH + MLA_Q_RANK + MLA_KV_RANK + MLA_ROPE + 2 * HG_KEY_WIDTH + 2 * HG_WIDTH
PEER_HEADS = 8
PEER_N_KEYS = 128
PEER_N_EXPERTS = PEER_N_KEYS * PEER_N_KEYS
PEER_TOPK = 16
PEER_QDIM = D_MODEL // 8
PEER_BLOCK = 128
NORM_EPS = 1e-6

kernel_name = "hymba_gmlp_mla_hgrn2_peer_trunk"

F32 = jnp.float32


def rmsnorm(x, g):
    xf = x.astype(F32)
    y = xf * lax.rsqrt(jnp.mean(xf * xf, axis=-1, keepdims=True) + NORM_EPS)
    return (y * g.astype(F32)).astype(x.dtype)


def head_rmsnorm(x, g, n_heads):
    shp = x.shape
    d = shp[-1] // n_heads
    y = rmsnorm(x.reshape(shp[:-1] + (n_heads, d)), g.reshape(n_heads, d))
    return y.reshape(shp)


def apply_rope(x, positions):
    half = MLA_ROPE // 2
    inv_freq = jnp.exp(-math.log(ROPE_THETA) * jnp.arange(half, dtype=F32) / half)
    ang = positions.astype(F32)[..., None] * inv_freq
    ang = ang.reshape(ang.shape[:2] + (1,) * (x.ndim - 3) + (half,))
    cos, sin = jnp.cos(ang), jnp.sin(ang)
    xf = x.astype(F32)
    x1, x2 = xf[..., :half], xf[..., half:]
    return jnp.concatenate([x1 * cos - x2 * sin, x2 * cos + x1 * sin], axis=-1).astype(x.dtype)


def chunked_gmlp(u, v, v_gain, w_s, b_s):
    B, S, _ = u.shape
    u = jax.nn.gelu(u)
    v = head_rmsnorm(jax.nn.gelu(v), v_gain, GM_HEADS)
    v = v.reshape(B, S // GM_CHUNK, GM_CHUNK, GM_HEADS, GM_HEAD_DIM)
    causal = jnp.tril(jnp.ones((GM_CHUNK, GM_CHUNK), dtype=bool))
    w = jnp.where(causal, w_s, 0).astype(v.dtype)
    z = jnp.einsum('hts,bcshd->bcthd', w, v) + b_s.T.astype(v.dtype)[None, None, :, :, None]
    return u * z.reshape(B, S, GM_WIDTH)


def mla(c_q, c_kv, k_rope, positions, q_norm, w_uq, kv_norm, w_ukv):
    B, S, _ = c_q.shape
    q = (rmsnorm(c_q, q_norm) @ w_uq).reshape(B, S, MLA_HEADS, MLA_NOPE + MLA_ROPE)
    q_nope = q[..., :MLA_NOPE]
    q_rope = apply_rope(q[..., MLA_NOPE:], positions)
    kv = (rmsnorm(c_kv, kv_norm) @ w_ukv).reshape(B, S, MLA_HEADS, MLA_NOPE + MLA_V)
    k_nope, v = kv[..., :MLA_NOPE], kv[..., MLA_NOPE:]
    k_rope = apply_rope(k_rope, positions)
    scale = (MLA_NOPE + MLA_ROPE) ** -0.5
    nb = S // ATTN_BLOCK
    qn_b = q_nope.reshape(B, nb, ATTN_BLOCK, MLA_HEADS, MLA_NOPE).transpose(1, 0, 2, 3, 4)
    qr_b = q_rope.reshape(B, nb, ATTN_BLOCK, MLA_HEADS, MLA_ROPE).transpose(1, 0, 2, 3, 4)
    k_pos = jnp.arange(S)

    def block(args):
        qn, qr, i = args
        s = jnp.einsum('bqhd,bkhd->bhqk', qn, k_nope) + jnp.einsum('bqhr,bkr->bhqk', qr, k_rope)
        s = s.astype(F32) * scale
        q_pos = i * ATTN_BLOCK + jnp.arange(ATTN_BLOCK)
        s = jnp.where(k_pos[None, :] <= q_pos[:, None], s, MASK_VALUE)
        p = jax.nn.softmax(s, axis=-1).astype(v.dtype)
        return jnp.einsum('bhqk,bkhd->bqhd', p, v)

    o = lax.map(block, (qn_b, qr_b, jnp.arange(nb)))
    return o.transpose(1, 0, 2, 3, 4).reshape(B, S, MLA_WIDTH)


def hgrn_lower_bounds(logits):
    p = jax.nn.softmax(logits.astype(F32), axis=0)
    return jnp.cumsum(p, axis=0) - p[0]


def hgrn2(q, f, i, g, lb, out_norm):
    B, S, _ = q.shape
    dt = i.dtype
    C, nc = HG_CHUNK, S // HG_CHUNK
    qf = jax.nn.silu(q.astype(F32)).reshape(B, S, HG_HEADS, HG_KEY)
    fx = f.astype(F32).reshape(B, S, HG_HEADS, HG_KEY)
    lbh = lb.reshape(HG_HEADS, HG_KEY)
    f_gate = lbh + (1.0 - lbh) * jax.nn.sigmoid(fx)
    log_f = jnp.log(jnp.maximum(f_gate, GATE_FLOOR))
    kf = (1.0 - lbh) * jax.nn.sigmoid(-fx)
    vf = i.astype(F32).reshape(B, S, HG_HEADS, HG_VAL)

    def to_chunks(t):
        return t.reshape(B, nc, C, HG_HEADS, t.shape[-1]).transpose(1, 0, 3, 2, 4)

    causal = jnp.tril(jnp.ones((C, C), dtype=bool))[:, :, None]

    def step(state, inp):
        qc, kc, vc, lfc = inp
        b = jnp.cumsum(lfc, axis=2)
        o_inter = jnp.einsum('bhtk,bhkv->bhtv', qc * jnp.exp(b), state)
        diff = b[:, :, :, None, :] - b[:, :, None, :, :]
        decay = jnp.where(causal, jnp.exp(jnp.where(causal, diff, 0.0)), 0.0)
        a = jnp.einsum('bhtk,bhsk,bhtsk->bhts', qc, kc, decay)
        o_intra = jnp.einsum('bhts,bhsv->bhtv', a, vc)
        b_last = b[:, :, -1:, :]
        new_state = jnp.exp(b_last[:, :, 0, :])[..., None] * state + \
            jnp.einsum('bhsk,bhsv->bhkv', kc * jnp.exp(b_last - b), vc)
        return new_state, o_inter + o_intra

    s0 = jnp.zeros((B, HG_HEADS, HG_KEY, HG_VAL), F32)
    _, o = lax.scan(step, s0, (to_chunks(qf), to_chunks(kf), to_chunks(vf), to_chunks(log_f)))
    o = o.transpose(1, 0, 3, 2, 4).reshape(B, S, HG_WIDTH)
    o = head_rmsnorm(o, out_norm, HG_HEADS) * jax.nn.silu(g.astype(F32))
    return o.astype(dt)


def peer(xn, w_q, sub_keys, u_tab, v_tab):
    B, S, D = xn.shape
    xb = xn.reshape((B * S) // PEER_BLOCK, PEER_BLOCK, D)
    K = PEER_TOPK

    def block(xt):
        q = (xt @ w_q).reshape(PEER_BLOCK, PEER_HEADS, 2, PEER_QDIM // 2)
        s = jnp.einsum('thpd,hpkd->thpk', q, sub_keys).astype(F32)
        s_top, i_top = lax.top_k(s, K)
        cand_s = (s_top[:, :, 0, :, None] + s_top[:, :, 1, None, :]).reshape(PEER_BLOCK, PEER_HEADS, K * K)
        cand_i = (i_top[:, :, 0, :, None] * PEER_N_KEYS + i_top[:, :, 1, None, :]).reshape(PEER_BLOCK, PEER_HEADS, K * K)
        best_s, pos = lax.top_k(cand_s, K)
        ids = jnp.take_along_axis(cand_i, pos, axis=-1)
        gate = jax.nn.softmax(best_s, axis=-1).astype(xt.dtype)
        u = jnp.take(u_tab, ids, axis=0)
        act = jax.nn.gelu(jnp.einsum('td,thkd->thk', xt, u))
        vv = jnp.take(v_tab, ids, axis=0)
        return jnp.einsum('thk,thkd->td', gate * act, vv)

    return lax.map(block, xb).reshape(B, S, D)


def setup_inputs(seed: int = 0) -> dict:
    key = jax.random.key(seed)
    ks = jax.random.split(key, 24)
    nrm = lambda k, shape, s: jax.random.normal(k, shape, F32) * s
    gain = lambda k, shape: 1.0 + 0.02 * jax.random.normal(k, shape, F32)
    x = jax.random.normal(ks[0], (BATCH, SEQ, D_MODEL), F32)
    offset = jax.random.randint(ks[1], (BATCH, 1), 0, SEQ, dtype=jnp.int32)
    positions = (jnp.arange(SEQ, dtype=jnp.int32)[None, :] + offset).astype(jnp.int32)
    return {
        "x": x,
        "positions": positions,
        "norm_mix": gain(ks[2], (DEPTH, D_MODEL)),
        "w_in": nrm(ks[3], (DEPTH, D_MODEL, IN_WIDTH), D_MODEL ** -0.5),
        "gm_v_norm": gain(ks[4], (DEPTH, GM_WIDTH)),
        "gm_ws": nrm(ks[5], (DEPTH, GM_HEADS, GM_CHUNK, GM_CHUNK), 0.5 * GM_CHUNK ** -0.5),
        "gm_b": 1.0 + 0.1 * jax.random.normal(ks[6], (DEPTH, GM_HEADS, GM_CHUNK), F32),
        "gm_out_norm": gain(ks[7], (DEPTH, GM_WIDTH)),
        "mla_q_norm": gain(ks[8], (DEPTH, MLA_Q_RANK)),
        "mla_w_uq": nrm(ks[9], (DEPTH, MLA_Q_RANK, MLA_HEADS * (MLA_NOPE + MLA_ROPE)), MLA_Q_RANK ** -0.5),
        "mla_kv_norm": gain(ks[10], (DEPTH, MLA_KV_RANK)),
        "mla_w_ukv": nrm(ks[11], (DEPTH, MLA_KV_RANK, MLA_HEADS * (MLA_NOPE + MLA_V)), MLA_KV_RANK ** -0.5),
        "mla_out_norm": gain(ks[12], (DEPTH, MLA_WIDTH)),
        "hg_lb_logits": nrm(ks[13], (DEPTH, HG_KEY_WIDTH), 0.5),
        "hg_out_norm": gain(ks[14], (DEPTH, HG_WIDTH)),
        "w_out": nrm(ks[15], (DEPTH, MIX_WIDTH, D_MODEL), MIX_WIDTH ** -0.5),
        "norm_ffn": gain(ks[16], (DEPTH, D_MODEL)),
        "peer_w_q": nrm(ks[17], (DEPTH, D_MODEL, PEER_HEADS * PEER_QDIM), D_MODEL ** -0.5),
        "peer_sub_keys": nrm(ks[18], (DEPTH, PEER_HEADS, 2, PEER_N_KEYS, PEER_QDIM // 2), (PEER_QDIM // 2) ** -0.5),
        "peer_u": nrm(ks[19], (DEPTH, PEER_N_EXPERTS, D_MODEL), D_MODEL ** -0.5),
        "peer_v": nrm(ks[20], (DEPTH, PEER_N_EXPERTS, D_MODEL), PEER_HEADS ** -0.5),
        "norm_final": gain(ks[21], (D_MODEL,)),
    }


def reference(x, positions, norm_mix, w_in, gm_v_norm, gm_ws, gm_b, gm_out_norm,
              mla_q_norm, mla_w_uq, mla_kv_norm, mla_w_ukv, mla_out_norm,
              hg_lb_logits, hg_out_norm, w_out, norm_ffn, peer_w_q, peer_sub_keys,
              peer_u, peer_v, norm_final):
    split_points = [int(s) for s in np.cumsum(IN_WIDTHS)[:-1]]
    lower_bounds = hgrn_lower_bounds(hg_lb_logits)
    for l in range(DEPTH):
        n = rmsnorm(x, norm_mix[l])
        proj = n @ w_in[l]
        u, v, c_q, c_kv, k_rope, hq, hf, hi, hg = jnp.split(proj, split_points, axis=-1)
        y_a = head_rmsnorm(chunked_gmlp(u, v, gm_v_norm[l], gm_ws[l], gm_b[l]), gm_out_norm[l], GM_HEADS)
        y_b = head_rmsnorm(mla(c_q, c_kv, k_rope, positions, mla_q_norm[l], mla_w_uq[l],
                               mla_kv_norm[l], mla_w_ukv[l]), mla_out_norm[l], MLA_HEADS)
        y_c = hgrn2(hq, hf, hi, hg, lower_bounds[l], hg_out_norm[l])
        x = x + jnp.concatenate([y_a, y_b, y_c], axis=-1) @ w_out[l]
        x = x + peer(rmsnorm(x, norm_ffn[l]), peer_w_q[l], peer_sub_keys[l], peer_u[l], peer_v[l])
    return rmsnorm(x, norm_final)
```

```python
import functools
import math

import numpy as np
import jax
import jax.numpy as jnp
from jax import lax
from jax.experimental import pallas as pl
from jax.experimental.pallas import tpu as pltpu

F32 = jnp.float32
BF16 = jnp.bfloat16
I32 = jnp.int32

D_MODEL = 1024
DEPTH = 4
GM_HEADS = 4
GM_WIDTH = 256
GM_CHUNK = 128
MLA_HEADS = 8
MLA_NOPE = 64
MLA_ROPE = 32
MLA_V = 64
MLA_WIDTH = 512
MLA_Q_RANK = 256
MLA_KV_RANK = 128
ROPE_THETA = 10000.0
MASK_VALUE = -1e30
HG_HEADS = 4
HG_WIDTH = 256
GATE_FLOOR = 1e-20
PEER_HEADS = 8
PEER_N_KEYS = 128
PEER_N_EXPERTS = PEER_N_KEYS * PEER_N_KEYS
PEER_TOPK = 16
PEER_QDIM = 128
NORM_EPS = 1e-6

LANES = 128
HEAD_DIM = 64
VMEM_LIMIT = 56 * 1024 * 1024

PROJ_WIDTH = 2176
COL_U, COL_V, COL_CQ, COL_HQ, COL_HF, COL_HI, COL_HG = 0, 256, 512, 768, 1024, 1280, 1536
COL_CKV, COL_KRA, COL_KRB = 1792, 1920, 2048

HG_CHUNK = 128
HG_LEVELS = 7

_NT = (((1,), (1,)), ((), ()))


def _cparams(sem):
    return pltpu.CompilerParams(dimension_semantics=sem, vmem_limit_bytes=VMEM_LIMIT)


def _dot(a, b):
    return jnp.dot(a, b, preferred_element_type=F32)


def _dot_nt(a, b):
    return lax.dot_general(a, b, _NT, preferred_element_type=F32)


def _split2(x):
    hi = x.astype(BF16)
    lo = (x - hi.astype(F32)).astype(BF16)
    return hi, lo


def _group_ones(width):
    r = lax.broadcasted_iota(I32, (width, width), 0) // HEAD_DIM
    c = lax.broadcasted_iota(I32, (width, width), 1) // HEAD_DIM
    return jnp.where(r == c, 1.0, 0.0).astype(BF16)


def _group_sum(x, ones_bd):
    hi, lo = _split2(x)
    return _dot(hi, ones_bd) + _dot(lo, ones_bd)


def _head_rmsnorm(x, gain, ones_bd):
    ms = _group_sum(x * x, ones_bd) * (1.0 / HEAD_DIM)
    return x * lax.rsqrt(ms + NORM_EPS) * gain


def _rmsnorm(x, gain):
    ms = jnp.mean(x * x, axis=-1, keepdims=True)
    return x * lax.rsqrt(ms + NORM_EPS) * gain


def _inproj_kernel(x_ref, g_ref, w_ref, wt_ref, o_ref, ot_ref):
    n = _rmsnorm(x_ref[...], g_ref[...]).astype(BF16)
    o_ref[...] = _dot(n, w_ref[...])
    ot_ref[...] = _dot_nt(wt_ref[...], n)


def _inproj(x, g, w, w_hit, tm=512):
    T = x.shape[0]
    return pl.pallas_call(
        _inproj_kernel,
        grid=(T // tm,),
        in_specs=[
            pl.BlockSpec((tm, D_MODEL), lambda i: (i, 0)),
            pl.BlockSpec((1, D_MODEL), lambda i: (0, 0)),
            pl.BlockSpec((D_MODEL, PROJ_WIDTH), lambda i: (0, 0)),
            pl.BlockSpec((HG_WIDTH, D_MODEL), lambda i: (0, 0)),
        ],
        out_specs=[
            pl.BlockSpec((tm, PROJ_WIDTH), lambda i: (i, 0)),
            pl.BlockSpec((HG_WIDTH, tm), lambda i: (0, i)),
        ],
        out_shape=[
            jax.ShapeDtypeStruct((T, PROJ_WIDTH), F32),
            jax.ShapeDtypeStruct((HG_WIDTH, T), F32),
        ],
        compiler_params=_cparams(("parallel",)),
        name="inproj",
    )(x, g, w, w_hit)


def _rope_table_kernel(pos_ref, c_ref, s_ref):
    lane = lax.broadcasted_iota(I32, (1, LANES), 1)
    half = MLA_ROPE // 2
    jf = ((lane - MLA_NOPE) & (half - 1)).astype(F32)
    inv_freq = jnp.exp(-math.log(ROPE_THETA) * jf / half)
    ang = pos_ref[...] * inv_freq
    c, s = jnp.cos(ang), jnp.sin(ang)
    in_rope = (lane >= MLA_NOPE) & (lane < MLA_NOPE + MLA_ROPE)
    c_ref[...] = jnp.where(lane < MLA_NOPE, 1.0, jnp.where(in_rope, c, 0.0))
    s_ref[...] = jnp.where(in_rope, jnp.where(lane < MLA_NOPE + half, -s, s), 0.0)


def _rope_tables(pos_b, tm=512):
    T = pos_b.shape[0]
    spec = pl.BlockSpec((tm, LANES), lambda i: (i, 0))
    return pl.pallas_call(
        _rope_table_kernel,
        grid=(T // tm,),
        in_specs=[spec],
        out_specs=[spec, spec],
        out_shape=[jax.ShapeDtypeStruct((T, LANES), F32)] * 2,
        compiler_params=_cparams(("parallel",)),
        name="rope_tables",
    )(pos_b)


def _gmlp_kernel(u_ref, v_ref, vg_ref, ws_ref, bias_ref, og_ref, o_ref):
    ones_bd = _group_ones(GM_WIDTH)
    u = jax.nn.gelu(u_ref[...])
    v = _head_rmsnorm(jax.nn.gelu(v_ref[...]), vg_ref[...], ones_bd)
    r = lax.broadcasted_iota(I32, (GM_CHUNK, GM_CHUNK), 0)
    c = lax.broadcasted_iota(I32, (GM_CHUNK, GM_CHUNK), 1)
    causal = c <= r
    lane = lax.broadcasted_iota(I32, (GM_CHUNK, LANES), 1)
    zs = []
    for pair in range(GM_HEADS // 2):
        vp = v[:, pair * LANES:(pair + 1) * LANES].astype(BF16)
        z0 = _dot(jnp.where(causal, ws_ref[2 * pair], 0.0).astype(BF16), vp)
        z1 = _dot(jnp.where(causal, ws_ref[2 * pair + 1], 0.0).astype(BF16), vp)
        zs.append(jnp.where(lane < HEAD_DIM, z0, z1))
    z = jnp.concatenate(zs, axis=1) + bias_ref[...]
    o_ref[...] = _head_rmsnorm(u * z, og_ref[...], ones_bd).astype(BF16)


def _gmlp(proj, v_gain, ws, bias2d, out_gain):
    T = proj.shape[0]
    return pl.pallas_call(
        _gmlp_kernel,
        grid=(T // GM_CHUNK,),
        in_specs=[
            pl.BlockSpec((GM_CHUNK, GM_WIDTH), lambda i: (i, COL_U // GM_WIDTH)),
            pl.BlockSpec((GM_CHUNK, GM_WIDTH), lambda i: (i, COL_V // GM_WIDTH)),
            pl.BlockSpec((1, GM_WIDTH), lambda i: (0, 0)),
            pl.BlockSpec((GM_HEADS, GM_CHUNK, GM_CHUNK), lambda i: (0, 0, 0)),
            pl.BlockSpec((GM_CHUNK, GM_WIDTH), lambda i: (0, 0)),
            pl.BlockSpec((1, GM_WIDTH), lambda i: (0, 0)),
        ],
        out_specs=pl.BlockSpec((GM_CHUNK, GM_WIDTH), lambda i: (i, 0)),
        out_shape=jax.ShapeDtypeStruct((T, GM_WIDTH), BF16),
        compiler_params=_cparams(("parallel",)),
        name="gmlp",
    )(proj, proj, v_gain, ws, bias2d, out_gain)


def _mla_proj_kernel(cq_ref, ckv_ref, kra_ref, krb_ref, c_ref, s_ref, qn_ref, kvn_ref,
                     wq_ref, wk_ref, wvt_ref, q_ref, k_ref, vt_ref):
    cos1, sin1 = c_ref[...], s_ref[...]
    cos8 = jnp.concatenate([cos1] * MLA_HEADS, axis=1)
    sin8 = jnp.concatenate([sin1] * MLA_HEADS, axis=1)
    nq = _rmsnorm(cq_ref[...], qn_ref[...]).astype(BF16)
    qq = _dot(nq, wq_ref[...])
    width = MLA_HEADS * LANES
    scale = (MLA_NOPE + MLA_ROPE) ** -0.5
    q_ref[...] = ((qq[:, :width] * cos8 + qq[:, width:] * sin8) * scale).astype(BF16)
    nkv = _rmsnorm(ckv_ref[...], kvn_ref[...]).astype(BF16)
    kr = kra_ref[...] * cos1 + krb_ref[...] * sin1
    k_ref[...] = (_dot(nkv, wk_ref[...]) + jnp.concatenate([kr] * MLA_HEADS, axis=1)).astype(BF16)
    vt_ref[...] = _dot_nt(wvt_ref[...], nkv).astype(BF16)


def _mla_proj(proj, cos_t, sin_t, q_norm, kv_norm, wq_cat, wk_pad, wv_t, tm=512):
    T = proj.shape[0]
    width = MLA_HEADS * LANES
    full = lambda shape: pl.BlockSpec(shape, lambda i: (0,) * len(shape))
    return pl.pallas_call(
        _mla_proj_kernel,
        grid=(T // tm,),
        in_specs=[
            pl.BlockSpec((tm, MLA_Q_RANK), lambda i: (i, COL_CQ // MLA_Q_RANK)),
            pl.BlockSpec((tm, LANES), lambda i: (i, COL_CKV // LANES)),
            pl.BlockSpec((tm, LANES), lambda i: (i, COL_KRA // LANES)),
            pl.BlockSpec((tm, LANES), lambda i: (i, COL_KRB // LANES)),
            pl.BlockSpec((tm, LANES), lambda i: (i, 0)),
            pl.BlockSpec((tm, LANES), lambda i: (i, 0)),
            full((1, MLA_Q_RANK)),
            full((1, MLA_KV_RANK)),
            full((MLA_Q_RANK, 2 * width)),
            full((MLA_KV_RANK, width)),
            full((MLA_WIDTH, MLA_KV_RANK)),
        ],
        out_specs=[
            pl.BlockSpec((tm, width), lambda i: (i, 0)),
            pl.BlockSpec((tm, width), lambda i: (i, 0)),
            pl.BlockSpec((MLA_WIDTH, tm), lambda i: (0, i)),
        ],
        out_shape=[
            jax.ShapeDtypeStruct((T, width), BF16),
            jax.ShapeDtypeStruct((T, width), BF16),
            jax.ShapeDtypeStruct((MLA_WIDTH, T), BF16),
        ],
        compiler_params=_cparams(("parallel",)),
        name="mla_proj",
    )(proj, proj, proj, proj, cos_t, sin_t, q_norm, kv_norm, wq_cat, wk_pad, wv_t)


def _attn_kernel(q_ref, k_ref, vt_ref, g_ref, o_ref, *, blk):
    qi = pl.program_id(2)
    kpos = lax.broadcasted_iota(I32, (blk, blk), 0)
    qpos = lax.broadcasted_iota(I32, (blk, blk), 1)
    gain = jnp.concatenate([g_ref[...]] * (blk // LANES), axis=1)
    outs = []
    for h in range(2):
        q = q_ref[:, h * LANES:(h + 1) * LANES]

        def step(j, carry, masked, h=h, q=q):
            m, l, acc = carry
            off = pl.multiple_of(j * blk, blk)
            kb = k_ref[pl.ds(off, blk), h * LANES:(h + 1) * LANES]
            st = _dot_nt(kb, q)
            if masked:
                st = jnp.where(kpos <= qpos, st, MASK_VALUE)
            m_new = jnp.maximum(m, jnp.max(st, axis=0, keepdims=True))
            p = jnp.exp(st - m_new)
            alpha = jnp.exp(m - m_new)
            l = alpha * l + jnp.sum(p, axis=0, keepdims=True)
            vt = vt_ref[h * MLA_V:(h + 1) * MLA_V, pl.ds(off, blk)]
            acc = alpha * acc + _dot(vt, p.astype(BF16))
            return m_new, l, acc

        init = (jnp.full((1, blk), -jnp.inf, F32), jnp.zeros((1, blk), F32), jnp.zeros((MLA_V, blk), F32))
        carry = lax.fori_loop(0, qi, functools.partial(step, masked=False), init)
        _, l, acc = step(qi, carry, True)
        o = acc / l
        ms = jnp.mean(o * o, axis=0, keepdims=True)
        outs.append(o * lax.rsqrt(ms + NORM_EPS) * gain[h * MLA_V:(h + 1) * MLA_V])
    o_ref[...] = jnp.concatenate(outs, axis=0).T.astype(BF16)


def _attention(q, k, vt, gain_b, batch, seq, blk=256):
    T = q.shape[0]
    nq = seq // blk
    return pl.pallas_call(
        functools.partial(_attn_kernel, blk=blk),
        grid=(batch, MLA_HEADS // 2, nq),
        in_specs=[
            pl.BlockSpec((blk, 2 * LANES), lambda b, hp, i: (b * nq + i, hp)),
            pl.BlockSpec((seq, 2 * LANES), lambda b, hp, i: (b, hp)),
            pl.BlockSpec((2 * MLA_V, seq), lambda b, hp, i: (hp, b)),
            pl.BlockSpec((2 * MLA_V, LANES), lambda b, hp, i: (hp, 0)),
        ],
        out_specs=pl.BlockSpec((blk, 2 * MLA_V), lambda b, hp, i: (b * nq + i, hp)),
        out_shape=jax.ShapeDtypeStruct((T, MLA_WIDTH), BF16),
        compiler_params=_cparams(("parallel", "parallel", "arbitrary")),
        name="mla_attention",
    )(q, k, vt, gain_b)


def _hgrn_sum_matrix():
    c = HG_CHUNK
    t = np.arange(c)[:, None]
    j = np.arange(c)[None, :]
    blocks = [(j <= t), (j > t)]
    for lv in range(HG_LEVELS):
        m = c >> (lv + 1)
        mid = (t // (2 * m)) * 2 * m + m - 1
        right = t > mid
        blocks.append(np.where(right, (j > mid) & (j <= t), (j > t) & (j <= mid)))
    mat = np.concatenate(blocks, axis=0).astype(np.float32)
    return np.concatenate([mat, mat, mat], axis=1)


def _hgrn_kernel(hq_ref, hf_ref, hi_ref, hit_ref, hg_ref, lb_ref, gain_ref, msum_ref, o_ref, st_ref, *, cb):
    c = HG_CHUNK

    @pl.when(pl.program_id(1) == 0)
    def _():
        st_ref[...] = jnp.zeros_like(st_ref)

    ones_bd = _group_ones(HG_WIDTH)
    lb = lb_ref[...]
    t_row = lax.broadcasted_iota(I32, (c, HG_WIDTH), 0)
    tt = lax.broadcasted_iota(I32, (c, c), 0)
    ss = lax.broadcasted_iota(I32, (c, c), 1)
    lane = lax.broadcasted_iota(I32, (c, LANES), 1)
    bd_mask = (tt // HEAD_DIM) == (ss // HEAD_DIM)
    level_masks = []
    for lv in range(HG_LEVELS):
        m = c >> (lv + 1)
        same_block = (tt // (2 * m)) == (ss // (2 * m))
        level_masks.append(same_block & ((tt & m) != 0) & ((ss & m) == 0))

    for ch in range(cb // c):
        rows = slice(ch * c, (ch + 1) * c)
        fx = hf_ref[rows, :]
        f_gate = lb + (1.0 - lb) * jax.nn.sigmoid(fx)
        lf = jnp.log(jnp.maximum(f_gate, GATE_FLOOR))
        kf = (1.0 - lb) * jax.nn.sigmoid(-fx)
        qf = jax.nn.silu(hq_ref[rows, :])
        vf = hi_ref[rows, :]
        lf_hi = lf.astype(BF16)
        r1 = lf - lf_hi.astype(F32)
        lf_mid = r1.astype(BF16)
        lf_lo = (r1 - lf_mid.astype(F32)).astype(BF16)
        e = jnp.exp(_dot(msum_ref[...], jnp.concatenate([lf_hi, lf_mid, lf_lo], axis=0)))
        eb = e[0:c]
        es = e[c:2 * c]
        q_in = (qf * eb).astype(BF16)
        k_out = (kf * es).astype(BF16)
        ys = []
        for lv in range(HG_LEVELS):
            m = c >> (lv + 1)
            ys.append(jnp.where((t_row & m) != 0, qf, kf) * e[(2 + lv) * c:(3 + lv) * c])
        o_parts = []
        for pair in range(HG_HEADS // 2):
            lanes = slice(pair * LANES, (pair + 1) * LANES)
            state_t = st_ref[pair]
            o_inter = _dot_nt(q_in[:, lanes], state_t.astype(BF16))
            a = [jnp.zeros((c, c), F32), jnp.zeros((c, c), F32)]
            for lv in range(HG_LEVELS):
                y = ys[lv][:, lanes]
                yb = y.astype(BF16)
                for h in range(2):
                    yh = jnp.where((lane // HEAD_DIM) == h, y, 0.0).astype(BF16)
                    a[h] = a[h] + jnp.where(level_masks[lv], _dot_nt(yh, yb), 0.0)
            vp = vf[:, lanes].astype(BF16)
            o_intra = jnp.where(lane < HEAD_DIM, _dot(a[0].astype(BF16), vp), _dot(a[1].astype(BF16), vp))
            o_parts.append(o_inter + o_intra)
            upd = _dot(hit_ref[lanes, rows].astype(BF16), k_out[:, lanes])
            st_ref[pair] = state_t * eb[c - 1:c, lanes] + jnp.where(bd_mask, upd, 0.0)
        o = jnp.concatenate(o_parts, axis=1) + _group_sum(qf * kf, ones_bd) * vf
        y_out = _head_rmsnorm(o, gain_ref[...], ones_bd) * jax.nn.silu(hg_ref[rows, :])
        o_ref[rows, :] = y_out.astype(BF16)


def _hgrn(proj, hit, lb, gain, msum, batch, seq, cb=512):
    T = proj.shape[0]
    nb = seq // cb
    col = lambda off: pl.BlockSpec((cb, HG_WIDTH), lambda b, i: (b * nb + i, off // HG_WIDTH))
    return pl.pallas_call(
        functools.partial(_hgrn_kernel, cb=cb),
        grid=(batch, nb),
        in_specs=[
            col(COL_HQ), col(COL_HF), col(COL_HI),
            pl.BlockSpec((HG_WIDTH, cb), lambda b, i: (0, b * nb + i)),
            col(COL_HG),
            pl.BlockSpec((1, HG_WIDTH), lambda b, i: (0, 0)),
            pl.BlockSpec((1, HG_WIDTH), lambda b, i: (0, 0)),
            pl.BlockSpec(msum.shape, lambda b, i: (0, 0)),
        ],
        out_specs=pl.BlockSpec((cb, HG_WIDTH), lambda b, i: (b * nb + i, 0)),
        out_shape=jax.ShapeDtypeStruct((T, HG_WIDTH), BF16),
        scratch_shapes=[pltpu.VMEM((HG_HEADS // 2, LANES, LANES), F32)],
        compiler_params=_cparams(("parallel", "arbitrary")),
        name="hgrn2",
    )(proj, proj, proj, hit, proj, lb, gain, msum)


def _outproj_kernel(ya_ref, yb_ref, yc_ref, x_ref, wa_ref, wb_ref, wc_ref, g_ref, wqt_ref, sk_ref,
                    x1_ref, xn_ref, s_ref):
    x1 = x_ref[...] + _dot(ya_ref[...], wa_ref[...]) + _dot(yb_ref[...], wb_ref[...]) + _dot(yc_ref[...], wc_ref[...])
    x1_ref[...] = x1
    xn = _rmsnorm(x1, g_ref[...]).astype(BF16)
    xn_ref[...] = xn
    qt = _dot_nt(wqt_ref[...], xn)
    half = PEER_QDIM // 2
    for hp in range(2 * PEER_HEADS):
        s_ref[hp] = _dot(sk_ref[hp], qt[hp * half:(hp + 1) * half].astype(BF16))


def _outproj(ya, yb, yc, x, wa, wb, wc, g, wq_t, sub_keys, tm=512):
    T = x.shape[0]
    full = lambda shape: pl.BlockSpec(shape, lambda i: (0,) * len(shape))
    row = lambda w: pl.BlockSpec((tm, w), lambda i: (i, 0))
    return pl.pallas_call(
        _outproj_kernel,
        grid=(T // tm,),
        in_specs=[
            row(GM_WIDTH), row(MLA_WIDTH), row(HG_WIDTH), row(D_MODEL),
            full((GM_WIDTH, D_MODEL)), full((MLA_WIDTH, D_MODEL)), full((HG_WIDTH, D_MODEL)),
            full((1, D_MODEL)), full((D_MODEL, D_MODEL)),
            full((2 * PEER_HEADS, PEER_N_KEYS, PEER_QDIM // 2)),
        ],
        out_specs=[
            row(D_MODEL), row(D_MODEL),
            pl.BlockSpec((2 * PEER_HEADS, PEER_N_KEYS, tm), lambda i: (0, 0, i)),
        ],
        out_shape=[
            jax.ShapeDtypeStruct((T, D_MODEL), F32),
            jax.ShapeDtypeStruct((T, D_MODEL), BF16),
            jax.ShapeDtypeStruct((2 * PEER_HEADS, PEER_N_KEYS, T), F32),
        ],
        compiler_params=_cparams(("parallel",)),
        name="outproj_peer_scores",
    )(ya, yb, yc, x, wa, wb, wc, g, wq_t, sub_keys)


_FRONTIER = [(r, c) for r in range(PEER_TOPK) for c in range(PEER_TOPK) if (r + 1) * (c + 1) <= PEER_TOPK]


def _topk_kernel(s_ref, ia_ref, ib_ref, g_ref, work_ref, val_ref, idx_ref):
    k_iota = lax.broadcasted_iota(I32, (PEER_N_KEYS, 8, LANES), 0)
    for p in range(2):
        work_ref[...] = s_ref[p]

        def extract(r, _, p=p):
            s = work_ref[...]
            m = jnp.max(s, axis=0)
            idx = jnp.min(jnp.where(s == m[None], k_iota, PEER_N_KEYS), axis=0)
            work_ref[...] = jnp.where(k_iota == idx[None], -jnp.inf, s)
            val_ref[p, r] = m
            idx_ref[p, r] = idx
            return 0

        lax.fori_loop(0, PEER_TOPK, extract, 0)

    va = [val_ref[0, r] for r in range(PEER_TOPK)]
    vb = [val_ref[1, r] for r in range(PEER_TOPK)]
    ia = [idx_ref[0, r] for r in range(PEER_TOPK)]
    ib = [idx_ref[1, r] for r in range(PEER_TOPK)]
    cand = [va[r] + vb[c] for r, c in _FRONTIER]
    cid = [ia[r] * PEER_N_KEYS + ib[c] for r, c in _FRONTIER]
    flat = [r * PEER_TOPK + c for r, c in _FRONTIER]
    best = []
    for k in range(PEER_TOPK):
        m = functools.reduce(jnp.maximum, cand)
        pos = functools.reduce(jnp.minimum, [jnp.where(cv == m, f, PEER_TOPK * PEER_TOPK) for cv, f in zip(cand, flat)])
        hit = [pos == f for f in flat]
        eid = functools.reduce(jnp.maximum, [jnp.where(hh, ci, 0) for hh, ci in zip(hit, cid)])
        cand = [jnp.where(hh, -jnp.inf, cv) for hh, cv in zip(hit, cand)]
        best.append(m)
        ia_ref[0, k] = eid >> 7
        ib_ref[0, k] = eid & (PEER_N_KEYS - 1)
    ex = [jnp.exp(b - best[0]) for b in best]
    inv = 1.0 / functools.reduce(jnp.add, ex)
    for k in range(PEER_TOPK):
        g_ref[0, k] = ex[k] * inv


def _topk(scores4):
    nt = scores4.shape[2]
    out_spec = pl.BlockSpec((1, PEER_TOPK, 8, LANES), lambda i, h: (h, 0, i, 0))
    out_sds = lambda dt: jax.ShapeDtypeStruct((PEER_HEADS, PEER_TOPK, nt, LANES), dt)
    return pl.pallas_call(
        _topk_kernel,
        grid=(nt // 8, PEER_HEADS),
        in_specs=[pl.BlockSpec((2, PEER_N_KEYS, 8, LANES), lambda i, h: (h, 0, i, 0))],
        out_specs=[out_spec, out_spec, out_spec],
        out_shape=[out_sds(I32), out_sds(I32), out_sds(F32)],
        scratch_shapes=[
            pltpu.VMEM((PEER_N_KEYS, 8, LANES), F32),
            pltpu.VMEM((2, PEER_TOPK, 8, LANES), F32),
            pltpu.VMEM((2, PEER_TOPK, 8, LANES), I32),
        ],
        compiler_params=_cparams(("parallel", "parallel")),
        name="peer_topk",
    )(scores4)


def _peer_kernel(xn_ref, x1_ref, a_ref, b_ref, g_ref, ut_ref, v_ref, o_ref, wt_ref, h_ref, *, tm, te, grp):
    j = pl.program_id(1)
    n_sub = te // PEER_N_KEYS

    @pl.when(j == 0)
    def _():
        o_ref[...] = x1_ref[...]
        key_iota = lax.broadcasted_iota(I32, (PEER_N_KEYS, LANES), 0)

        def build(i, _):
            t0 = pl.multiple_of(i * grp, grp)
            ws = []
            for r in range(grp):
                a_row = a_ref[pl.ds(t0 + r, 1), :]
                b_row = b_ref[pl.ds(t0 + r, 1), :]
                g_row = g_ref[pl.ds(t0 + r, 1), :]
                oa = jnp.where(key_iota == a_row, 1.0, 0.0).astype(BF16)
                gob = jnp.where(key_iota == b_row, g_row, 0.0).astype(BF16)
                ws.append(_dot_nt(oa, gob))
            wt = pltpu.einshape("tij->itj", jnp.stack(ws, axis=0))
            wt_ref[:, pl.ds(t0, grp), :] = wt.astype(BF16)
            return 0

        lax.fori_loop(0, tm // grp, build, 0)

    act = jax.nn.gelu(_dot(xn_ref[...], ut_ref[...])).astype(BF16)
    for i in range(n_sub):
        h_ref[:, i * LANES:(i + 1) * LANES] = act[:, i * LANES:(i + 1) * LANES] * wt_ref[j * n_sub + i]
    o_ref[...] += _dot(h_ref[...], v_ref[...])


def _peer(xn, x1, ids_a, ids_b, gates, u_t, v_tab, tm=512, te=1024, grp=16):
    T = xn.shape[0]
    row = lambda w: pl.BlockSpec((tm, w), lambda i, j: (i, 0))
    return pl.pallas_call(
        functools.partial(_peer_kernel, tm=tm, te=te, grp=grp),
        grid=(T // tm, PEER_N_EXPERTS // te),
        in_specs=[
            row(D_MODEL), row(D_MODEL), row(LANES), row(LANES), row(LANES),
            pl.BlockSpec((D_MODEL, te), lambda i, j: (0, j)),
            pl.BlockSpec((te, D_MODEL), lambda i, j: (j, 0)),
        ],
        out_specs=row(D_MODEL),
        out_shape=jax.ShapeDtypeStruct((T, D_MODEL), F32),
        scratch_shapes=[
            pltpu.VMEM((PEER_N_KEYS, tm, LANES), BF16),
            pltpu.VMEM((tm, te), BF16),
        ],
        compiler_params=_cparams(("parallel", "arbitrary")),
        name="peer_mix",
    )(xn, x1, ids_a, ids_b, gates, u_t, v_tab)


def _final_norm_kernel(x_ref, g_ref, o_ref):
    o_ref[...] = _rmsnorm(x_ref[...], g_ref[...])


def _final_norm(x, g, tm=512):
    T = x.shape[0]
    return pl.pallas_call(
        _final_norm_kernel,
        grid=(T // tm,),
        in_specs=[pl.BlockSpec((tm, D_MODEL), lambda i: (i, 0)), pl.BlockSpec((1, D_MODEL), lambda i: (0, 0))],
        out_specs=pl.BlockSpec((tm, D_MODEL), lambda i: (i, 0)),
        out_shape=jax.ShapeDtypeStruct((T, D_MODEL), F32),
        compiler_params=_cparams(("parallel",)),
        name="final_norm",
    )(x, g)


def _prep_w_in(w):
    u, v, cq = w[:, 0:256], w[:, 256:512], w[:, 512:768]
    ckv, kr = w[:, 768:896], w[:, 896:928]
    hq, hf, hi, hg = w[:, 928:1184], w[:, 1184:1440], w[:, 1440:1696], w[:, 1696:1952]
    half = MLA_ROPE // 2
    kr_sw = jnp.concatenate([kr[:, half:], kr[:, :half]], axis=1)
    z64 = jnp.zeros((D_MODEL, MLA_NOPE), w.dtype)
    z32 = jnp.zeros((D_MODEL, LANES - MLA_NOPE - MLA_ROPE), w.dtype)
    kra = jnp.concatenate([z64, kr, z32], axis=1)
    krb = jnp.concatenate([z64, kr_sw, z32], axis=1)
    w_perm = jnp.concatenate([u, v, cq, hq, hf, hi, hg, ckv, kra, krb], axis=1).astype(BF16)
    return w_perm, hi.T.astype(BF16)


def _prep_w_uq(w):
    half = MLA_ROPE // 2
    w3 = w.reshape(MLA_Q_RANK, MLA_HEADS, MLA_NOPE + MLA_ROPE)
    nope, rope = w3[:, :, :MLA_NOPE], w3[:, :, MLA_NOPE:]
    rope_sw = jnp.concatenate([rope[:, :, half:], rope[:, :, :half]], axis=-1)
    z32 = jnp.zeros((MLA_Q_RANK, MLA_HEADS, LANES - MLA_NOPE - MLA_ROPE), w.dtype)
    z64 = jnp.zeros((MLA_Q_RANK, MLA_HEADS, MLA_NOPE), w.dtype)
    main = jnp.concatenate([nope, rope, z32], axis=-1).reshape(MLA_Q_RANK, MLA_HEADS * LANES)
    swap = jnp.concatenate([z64, rope_sw, z32], axis=-1).reshape(MLA_Q_RANK, MLA_HEADS * LANES)
    return jnp.concatenate([main, swap], axis=1).astype(BF16)


def _prep_w_ukv(w):
    w3 = w.reshape(MLA_KV_RANK, MLA_HEADS, MLA_NOPE + MLA_V)
    k_nope, v = w3[:, :, :MLA_NOPE], w3[:, :, MLA_NOPE:]
    z = jnp.zeros((MLA_KV_RANK, MLA_HEADS, LANES - MLA_NOPE), w.dtype)
    wk_pad = jnp.concatenate([k_nope, z], axis=-1).reshape(MLA_KV_RANK, MLA_HEADS * LANES).astype(BF16)
    wv_t = v.reshape(MLA_KV_RANK, MLA_WIDTH).T.astype(BF16)
    return wk_pad, wv_t


def kernel(x, positions, norm_mix, w_in, gm_v_norm, gm_ws, gm_b, gm_out_norm, mla_q_norm, mla_w_uq, mla_kv_norm, mla_w_ukv, mla_out_norm, hg_lb_logits, hg_out_norm, w_out, norm_ffn, peer_w_q, peer_sub_keys, peer_u, peer_v, norm_final):
    batch, seq, _ = x.shape
    T = batch * seq
    depth = w_in.shape[0]
    xt = x.reshape(T, D_MODEL)
    pos_b = jnp.broadcast_to(positions.reshape(T, 1).astype(F32), (T, LANES))
    cos_t, sin_t = _rope_tables(pos_b)
    p = jax.nn.softmax(hg_lb_logits.astype(F32), axis=0)
    lower = jnp.cumsum(p, axis=0) - p[0]
    msum = jnp.asarray(_hgrn_sum_matrix(), BF16)
    row = lambda v: v.reshape(1, -1)

    for l in range(depth):
        w_perm, w_hit = _prep_w_in(w_in[l])
        proj, hit = _inproj(xt, row(norm_mix[l]), w_perm, w_hit)
        bias2d = jnp.repeat(gm_b[l].T, HEAD_DIM, axis=1)
        ya = _gmlp(proj, row(gm_v_norm[l]), gm_ws[l], bias2d, row(gm_out_norm[l]))
        wk_pad, wv_t = _prep_w_ukv(mla_w_ukv[l])
        q, k, vt = _mla_proj(proj, cos_t, sin_t, row(mla_q_norm[l]), row(mla_kv_norm[l]),
                             _prep_w_uq(mla_w_uq[l]), wk_pad, wv_t)
        gain_b = jnp.broadcast_to(mla_out_norm[l][:, None], (MLA_WIDTH, LANES))
        yb = _attention(q, k, vt, gain_b, batch, seq)
        yc = _hgrn(proj, hit, row(lower[l]), row(hg_out_norm[l]), msum, batch, seq)
        wo = w_out[l].astype(BF16)
        sub_keys = peer_sub_keys[l].reshape(2 * PEER_HEADS, PEER_N_KEYS, PEER_QDIM // 2).astype(BF16)
        x1, xn, scores = _outproj(ya, yb, yc, xt, wo[:GM_WIDTH], wo[GM_WIDTH:GM_WIDTH + MLA_WIDTH],
                                  wo[GM_WIDTH + MLA_WIDTH:], row(norm_ffn[l]), peer_w_q[l].T.astype(BF16), sub_keys)
        ia, ib, gt = _topk(scores.reshape(2 * PEER_HEADS, PEER_N_KEYS, T // LANES, LANES))
        to_rows = lambda a: a.reshape(PEER_HEADS * PEER_TOPK, T).T
        xt = _peer(xn, x1, to_rows(ia), to_rows(ib), to_rows(gt),
                   peer_u[l].T.astype(BF16), peer_v[l].astype(BF16))
    return _final_norm(xt, row(norm_final)).reshape(batch, seq, D_MODEL)
```

```python
import functools
import math

import numpy as np
import jax
import jax.numpy as jnp
from jax import lax
from jax.experimental import pallas as pl
from jax.experimental.pallas import tpu as pltpu

F32 = jnp.float32
BF16 = jnp.bfloat16
I32 = jnp.int32

D_MODEL = 1024
DEPTH = 4
GM_HEADS = 4
GM_WIDTH = 256
GM_CHUNK = 128
MLA_HEADS = 8
MLA_NOPE = 64
MLA_ROPE = 32
MLA_V = 64
MLA_WIDTH = 512
MLA_Q_RANK = 256
MLA_KV_RANK = 128
ROPE_THETA = 10000.0
MASK_VALUE = -1e30
HG_HEADS = 4
HG_WIDTH = 256
GATE_FLOOR = 1e-20
PEER_HEADS = 8
PEER_N_KEYS = 128
PEER_N_EXPERTS = PEER_N_KEYS * PEER_N_KEYS
PEER_TOPK = 16
PEER_QDIM = 128
NORM_EPS = 1e-6

LANES = 128
HEAD_DIM = 64
VMEM_LIMIT = 56 * 1024 * 1024

PROJ_WIDTH = 2176
COL_U, COL_V, COL_CQ, COL_HQ, COL_HF, COL_HI, COL_HG = 0, 256, 512, 768, 1024, 1280, 1536
COL_CKV, COL_KRA, COL_KRB = 1792, 1920, 2048

HG_CHUNK = 128
HG_LEVELS = 7

_NT = (((1,), (1,)), ((), ()))


def _cparams(sem):
    return pltpu.CompilerParams(dimension_semantics=sem, vmem_limit_bytes=VMEM_LIMIT)


def _dot(a, b):
    return jnp.dot(a, b, preferred_element_type=F32)


def _dot_nt(a, b):
    return lax.dot_general(a, b, _NT, preferred_element_type=F32)


def _split2(x):
    hi = x.astype(BF16)
    lo = (x - hi.astype(F32)).astype(BF16)
    return hi, lo


def _group_ones(width):
    r = lax.broadcasted_iota(I32, (width, width), 0) // HEAD_DIM
    c = lax.broadcasted_iota(I32, (width, width), 1) // HEAD_DIM
    return jnp.where(r == c, 1.0, 0.0).astype(BF16)


def _group_sum(x, ones_bd):
    hi, lo = _split2(x)
    return _dot(hi, ones_bd) + _dot(lo, ones_bd)


def _head_rmsnorm(x, gain, ones_bd):
    ms = _group_sum(x * x, ones_bd) * (1.0 / HEAD_DIM)
    return x * lax.rsqrt(ms + NORM_EPS) * gain


def _rmsnorm(x, gain):
    ms = jnp.mean(x * x, axis=-1, keepdims=True)
    return x * lax.rsqrt(ms + NORM_EPS) * gain


def _inproj_kernel(x_ref, g_ref, w_ref, wt_ref, o_ref, ot_ref):
    n = _rmsnorm(x_ref[...], g_ref[...]).astype(BF16)
    o_ref[...] = _dot(n, w_ref[...])
    ot_ref[...] = _dot_nt(wt_ref[...], n)


def _inproj(x, g, w, w_hit, tm=512):
    T = x.shape[0]
    return pl.pallas_call(
        _inproj_kernel,
        grid=(T // tm,),
        in_specs=[
            pl.BlockSpec((tm, D_MODEL), lambda i: (i, 0)),
            pl.BlockSpec((1, D_MODEL), lambda i: (0, 0)),
            pl.BlockSpec((D_MODEL, PROJ_WIDTH), lambda i: (0, 0)),
            pl.BlockSpec((HG_WIDTH, D_MODEL), lambda i: (0, 0)),
        ],
        out_specs=[
            pl.BlockSpec((tm, PROJ_WIDTH), lambda i: (i, 0)),
            pl.BlockSpec((HG_WIDTH, tm), lambda i: (0, i)),
        ],
        out_shape=[
            jax.ShapeDtypeStruct((T, PROJ_WIDTH), F32),
            jax.ShapeDtypeStruct((HG_WIDTH, T), F32),
        ],
        compiler_params=_cparams(("parallel",)),
        name="inproj",
    )(x, g, w, w_hit)


def _rope_table_kernel(pos_ref, c_ref, s_ref):
    lane = lax.broadcasted_iota(I32, (1, LANES), 1)
    half = MLA_ROPE // 2
    jf = ((lane - MLA_NOPE) & (half - 1)).astype(F32)
    inv_freq = jnp.exp(-math.log(ROPE_THETA) * jf / half)
    ang = pos_ref[...] * inv_freq
    c, s = jnp.cos(ang), jnp.sin(ang)
    in_rope = (lane >= MLA_NOPE) & (lane < MLA_NOPE + MLA_ROPE)
    c_ref[...] = jnp.where(lane < MLA_NOPE, 1.0, jnp.where(in_rope, c, 0.0))
    s_ref[...] = jnp.where(in_rope, jnp.where(lane < MLA_NOPE + half, -s, s), 0.0)


def _rope_tables(pos_b, tm=512):
    T = pos_b.shape[0]
    spec = pl.BlockSpec((tm, LANES), lambda i: (i, 0))
    return pl.pallas_call(
        _rope_table_kernel,
        grid=(T // tm,),
        in_specs=[spec],
        out_specs=[spec, spec],
        out_shape=[jax.ShapeDtypeStruct((T, LANES), F32)] * 2,
        compiler_params=_cparams(("parallel",)),
        name="rope_tables",
    )(pos_b)


def _gmlp_kernel(u_ref, v_ref, vg_ref, ws_ref, bias_ref, og_ref, o_ref):
    ones_bd = _group_ones(GM_WIDTH)
    u = jax.nn.gelu(u_ref[...])
    v = _head_rmsnorm(jax.nn.gelu(v_ref[...]), vg_ref[...], ones_bd)
    r = lax.broadcasted_iota(I32, (GM_CHUNK, GM_CHUNK), 0)
    c = lax.broadcasted_iota(I32, (GM_CHUNK, GM_CHUNK), 1)
    causal = c <= r
    lane = lax.broadcasted_iota(I32, (GM_CHUNK, LANES), 1)
    zs = []
    for pair in range(GM_HEADS // 2):
        vp = v[:, pair * LANES:(pair + 1) * LANES].astype(BF16)
        z0 = _dot(jnp.where(causal, ws_ref[2 * pair], 0.0).astype(BF16), vp)
        z1 = _dot(jnp.where(causal, ws_ref[2 * pair + 1], 0.0).astype(BF16), vp)
        zs.append(jnp.where(lane < HEAD_DIM, z0, z1))
    z = jnp.concatenate(zs, axis=1) + bias_ref[...]
    o_ref[...] = _head_rmsnorm(u * z, og_ref[...], ones_bd).astype(BF16)


def _gmlp(proj, v_gain, ws, bias2d, out_gain):
    T = proj.shape[0]
    return pl.pallas_call(
        _gmlp_kernel,
        grid=(T // GM_CHUNK,),
        in_specs=[
            pl.BlockSpec((GM_CHUNK, GM_WIDTH), lambda i: (i, COL_U // GM_WIDTH)),
            pl.BlockSpec((GM_CHUNK, GM_WIDTH), lambda i: (i, COL_V // GM_WIDTH)),
            pl.BlockSpec((1, GM_WIDTH), lambda i: (0, 0)),
            pl.BlockSpec((GM_HEADS, GM_CHUNK, GM_CHUNK), lambda i: (0, 0, 0)),
            pl.BlockSpec((GM_CHUNK, GM_WIDTH), lambda i: (0, 0)),
            pl.BlockSpec((1, GM_WIDTH), lambda i: (0, 0)),
        ],
        out_specs=pl.BlockSpec((GM_CHUNK, GM_WIDTH), lambda i: (i, 0)),
        out_shape=jax.ShapeDtypeStruct((T, GM_WIDTH), BF16),
        compiler_params=_cparams(("parallel",)),
        name="gmlp",
    )(proj, proj, v_gain, ws, bias2d, out_gain)


def _mla_proj_kernel(cq_ref, ckv_ref, kra_ref, krb_ref, c_ref, s_ref, qn_ref, kvn_ref,
                     wq_ref, wk_ref, wvt_ref, q_ref, k_ref, vt_ref):
    cos1, sin1 = c_ref[...], s_ref[...]
    cos8 = jnp.concatenate([cos1] * MLA_HEADS, axis=1)
    sin8 = jnp.concatenate([sin1] * MLA_HEADS, axis=1)
    nq = _rmsnorm(cq_ref[...], qn_ref[...]).astype(BF16)
    qq = _dot(nq, wq_ref[...])
    width = MLA_HEADS * LANES
    scale = (MLA_NOPE + MLA_ROPE) ** -0.5
    q_ref[...] = ((qq[:, :width] * cos8 + qq[:, width:] * sin8) * scale).astype(BF16)
    nkv = _rmsnorm(ckv_ref[...], kvn_ref[...]).astype(BF16)
    kr = kra_ref[...] * cos1 + krb_ref[...] * sin1
    k_ref[...] = (_dot(nkv, wk_ref[...]) + jnp.concatenate([kr] * MLA_HEADS, axis=1)).astype(BF16)
    vt_ref[...] = _dot_nt(wvt_ref[...], nkv).astype(BF16)


def _mla_proj(proj, cos_t, sin_t, q_norm, kv_norm, wq_cat, wk_pad, wv_t, tm=512):
    T = proj.shape[0]
    width = MLA_HEADS * LANES
    full = lambda shape: pl.BlockSpec(shape, lambda i: (0,) * len(shape))
    return pl.pallas_call(
        _mla_proj_kernel,
        grid=(T // tm,),
        in_specs=[
            pl.BlockSpec((tm, MLA_Q_RANK), lambda i: (i, COL_CQ // MLA_Q_RANK)),
            pl.BlockSpec((tm, LANES), lambda i: (i, COL_CKV // LANES)),
            pl.BlockSpec((tm, LANES), lambda i: (i, COL_KRA // LANES)),
            pl.BlockSpec((tm, LANES), lambda i: (i, COL_KRB // LANES)),
            pl.BlockSpec((tm, LANES), lambda i: (i, 0)),
            pl.BlockSpec((tm, LANES), lambda i: (i, 0)),
            full((1, MLA_Q_RANK)),
            full((1, MLA_KV_RANK)),
            full((MLA_Q_RANK, 2 * width)),
            full((MLA_KV_RANK, width)),
            full((MLA_WIDTH, MLA_KV_RANK)),
        ],
        out_specs=[
            pl.BlockSpec((tm, width), lambda i: (i, 0)),
            pl.BlockSpec((tm, width), lambda i: (i, 0)),
            pl.BlockSpec((MLA_WIDTH, tm), lambda i: (0, i)),
        ],
        out_shape=[
            jax.ShapeDtypeStruct((T, width), BF16),
            jax.ShapeDtypeStruct((T, width), BF16),
            jax.ShapeDtypeStruct((MLA_WIDTH, T), BF16),
        ],
        compiler_params=_cparams(("parallel",)),
        name="mla_proj",
    )(proj, proj, proj, proj, cos_t, sin_t, q_norm, kv_norm, wq_cat, wk_pad, wv_t)


def _attn_kernel(q_ref, k_ref, vt_ref, g_ref, o_ref, *, blk):
    qi = pl.program_id(2)
    kpos = lax.broadcasted_iota(I32, (blk, blk), 0)
    qpos = lax.broadcasted_iota(I32, (blk, blk), 1)
    gain = jnp.concatenate([g_ref[...]] * (blk // LANES), axis=1)
    qs = [q_ref[:, h * LANES:(h + 1) * LANES] for h in range(2)]

    def step(j, carry, masked):
        off = pl.multiple_of(j * blk, blk)
        new = []
        for h in range(2):
            m, l, acc = carry[3 * h:3 * h + 3]
            kb = k_ref[pl.ds(off, blk), h * LANES:(h + 1) * LANES]
            st = _dot_nt(kb, qs[h])
            if masked:
                st = jnp.where(kpos <= qpos, st, MASK_VALUE)
            m_new = jnp.maximum(m, jnp.max(st, axis=0, keepdims=True))
            p = jnp.exp(st - m_new)
            alpha = jnp.exp(m - m_new)
            l = alpha * l + jnp.sum(p, axis=0, keepdims=True)
            vt = vt_ref[h * MLA_V:(h + 1) * MLA_V, pl.ds(off, blk)]
            acc = alpha * acc + _dot(vt, p.astype(BF16))
            new += [m_new, l, acc]
        return tuple(new)

    init = (jnp.full((1, blk), -jnp.inf, F32), jnp.zeros((1, blk), F32), jnp.zeros((MLA_V, blk), F32)) * 2
    carry = lax.fori_loop(0, qi, functools.partial(step, masked=False), init)
    carry = step(qi, carry, True)
    outs = []
    for h in range(2):
        _, l, acc = carry[3 * h:3 * h + 3]
        o = acc / l
        ms = jnp.mean(o * o, axis=0, keepdims=True)
        outs.append(o * lax.rsqrt(ms + NORM_EPS) * gain[h * MLA_V:(h + 1) * MLA_V])
    o_ref[...] = jnp.concatenate(outs, axis=0).T.astype(BF16)


def _attention(q, k, vt, gain_b, batch, seq, blk=512):
    T = q.shape[0]
    nq = seq // blk
    return pl.pallas_call(
        functools.partial(_attn_kernel, blk=blk),
        grid=(batch, MLA_HEADS // 2, nq),
        in_specs=[
            pl.BlockSpec((blk, 2 * LANES), lambda b, hp, i: (b * nq + i, hp)),
            pl.BlockSpec((seq, 2 * LANES), lambda b, hp, i: (b, hp)),
            pl.BlockSpec((2 * MLA_V, seq), lambda b, hp, i: (hp, b)),
            pl.BlockSpec((2 * MLA_V, LANES), lambda b, hp, i: (hp, 0)),
        ],
        out_specs=pl.BlockSpec((blk, 2 * MLA_V), lambda b, hp, i: (b * nq + i, hp)),
        out_shape=jax.ShapeDtypeStruct((T, MLA_WIDTH), BF16),
        compiler_params=_cparams(("parallel", "parallel", "arbitrary")),
        name="mla_attention",
    )(q, k, vt, gain_b)


def _hgrn_sum_matrix():
    c = HG_CHUNK
    t = np.arange(c)[:, None]
    j = np.arange(c)[None, :]
    blocks = [(j <= t), (j > t)]
    for lv in range(HG_LEVELS):
        m = c >> (lv + 1)
        mid = (t // (2 * m)) * 2 * m + m - 1
        right = t > mid
        blocks.append(np.where(right, (j > mid) & (j <= t), (j > t) & (j <= mid)))
    mat = np.concatenate(blocks, axis=0).astype(np.float32)
    return np.concatenate([mat, mat, mat], axis=1)


def _hgrn_kernel(hq_ref, hf_ref, hi_ref, hit_ref, hg_ref, lb_ref, gain_ref, msum_ref, o_ref, st_ref, *, cb):
    c = HG_CHUNK

    @pl.when(pl.program_id(1) == 0)
    def _():
        st_ref[...] = jnp.zeros_like(st_ref)

    ones_bd = _group_ones(HG_WIDTH)
    lb = lb_ref[...]
    t_row = lax.broadcasted_iota(I32, (c, HG_WIDTH), 0)
    tt = lax.broadcasted_iota(I32, (c, c), 0)
    ss = lax.broadcasted_iota(I32, (c, c), 1)
    lane = lax.broadcasted_iota(I32, (c, LANES), 1)
    bd_mask = (tt // HEAD_DIM) == (ss // HEAD_DIM)
    level_masks = []
    for lv in range(HG_LEVELS):
        m = c >> (lv + 1)
        same_block = (tt // (2 * m)) == (ss // (2 * m))
        level_masks.append(same_block & ((tt & m) != 0) & ((ss & m) == 0))

    for ch in range(cb // c):
        rows = slice(ch * c, (ch + 1) * c)
        fx = hf_ref[rows, :]
        f_gate = lb + (1.0 - lb) * jax.nn.sigmoid(fx)
        lf = jnp.log(jnp.maximum(f_gate, GATE_FLOOR))
        kf = (1.0 - lb) * jax.nn.sigmoid(-fx)
        qf = jax.nn.silu(hq_ref[rows, :])
        vf = hi_ref[rows, :]
        lf_hi = lf.astype(BF16)
        r1 = lf - lf_hi.astype(F32)
        lf_mid = r1.astype(BF16)
        lf_lo = (r1 - lf_mid.astype(F32)).astype(BF16)
        e = jnp.exp(_dot(msum_ref[...], jnp.concatenate([lf_hi, lf_mid, lf_lo], axis=0)))
        eb = e[0:c]
        es = e[c:2 * c]
        q_in = (qf * eb).astype(BF16)
        k_out = (kf * es).astype(BF16)
        ys = []
        for lv in range(HG_LEVELS):
            m = c >> (lv + 1)
            ys.append(jnp.where((t_row & m) != 0, qf, kf) * e[(2 + lv) * c:(3 + lv) * c])
        o_parts = []
        for pair in range(HG_HEADS // 2):
            lanes = slice(pair * LANES, (pair + 1) * LANES)
            state_t = st_ref[pair]
            o_inter = _dot_nt(q_in[:, lanes], state_t.astype(BF16))
            a = [jnp.zeros((c, c), F32), jnp.zeros((c, c), F32)]
            for lv in range(HG_LEVELS):
                y = ys[lv][:, lanes]
                yb = y.astype(BF16)
                for h in range(2):
                    yh = jnp.where((lane // HEAD_DIM) == h, y, 0.0).astype(BF16)
                    a[h] = a[h] + jnp.where(level_masks[lv], _dot_nt(yh, yb), 0.0)
            vp = vf[:, lanes].astype(BF16)
            o_intra = jnp.where(lane < HEAD_DIM, _dot(a[0].astype(BF16), vp), _dot(a[1].astype(BF16), vp))
            o_parts.append(o_inter + o_intra)
            upd = _dot(hit_ref[lanes, rows].astype(BF16), k_out[:, lanes])
            st_ref[pair] = state_t * eb[c - 1:c, lanes] + jnp.where(bd_mask, upd, 0.0)
        o = jnp.concatenate(o_parts, axis=1) + _group_sum(qf * kf, ones_bd) * vf
        y_out = _head_rmsnorm(o, gain_ref[...], ones_bd) * jax.nn.silu(hg_ref[rows, :])
        o_ref[rows, :] = y_out.astype(BF16)


def _hgrn(proj, hit, lb, gain, msum, batch, seq, cb=512):
    T = proj.shape[0]
    nb = seq // cb
    col = lambda off: pl.BlockSpec((cb, HG_WIDTH), lambda b, i: (b * nb + i, off // HG_WIDTH))
    return pl.pallas_call(
        functools.partial(_hgrn_kernel, cb=cb),
        grid=(batch, nb),
        in_specs=[
            col(COL_HQ), col(COL_HF), col(COL_HI),
            pl.BlockSpec((HG_WIDTH, cb), lambda b, i: (0, b * nb + i)),
            col(COL_HG),
            pl.BlockSpec((1, HG_WIDTH), lambda b, i: (0, 0)),
            pl.BlockSpec((1, HG_WIDTH), lambda b, i: (0, 0)),
            pl.BlockSpec(msum.shape, lambda b, i: (0, 0)),
        ],
        out_specs=pl.BlockSpec((cb, HG_WIDTH), lambda b, i: (b * nb + i, 0)),
        out_shape=jax.ShapeDtypeStruct((T, HG_WIDTH), BF16),
        scratch_shapes=[pltpu.VMEM((HG_HEADS // 2, LANES, LANES), F32)],
        compiler_params=_cparams(("parallel", "arbitrary")),
        name="hgrn2",
    )(proj, proj, proj, hit, proj, lb, gain, msum)


def _outproj_kernel(ya_ref, yb_ref, yc_ref, x_ref, wa_ref, wb_ref, wc_ref, g_ref, wqt_ref, sk_ref,
                    x1_ref, xn_ref, s_ref):
    x1 = x_ref[...] + _dot(ya_ref[...], wa_ref[...]) + _dot(yb_ref[...], wb_ref[...]) + _dot(yc_ref[...], wc_ref[...])
    x1_ref[...] = x1
    xn = _rmsnorm(x1, g_ref[...]).astype(BF16)
    xn_ref[...] = xn
    qt = _dot_nt(wqt_ref[...], xn)
    half = PEER_QDIM // 2
    for hp in range(2 * PEER_HEADS):
        s_ref[hp] = _dot(sk_ref[hp], qt[hp * half:(hp + 1) * half].astype(BF16))


def _outproj(ya, yb, yc, x, wa, wb, wc, g, wq_t, sub_keys, tm=512):
    T = x.shape[0]
    full = lambda shape: pl.BlockSpec(shape, lambda i: (0,) * len(shape))
    row = lambda w: pl.BlockSpec((tm, w), lambda i: (i, 0))
    return pl.pallas_call(
        _outproj_kernel,
        grid=(T // tm,),
        in_specs=[
            row(GM_WIDTH), row(MLA_WIDTH), row(HG_WIDTH), row(D_MODEL),
            full((GM_WIDTH, D_MODEL)), full((MLA_WIDTH, D_MODEL)), full((HG_WIDTH, D_MODEL)),
            full((1, D_MODEL)), full((D_MODEL, D_MODEL)),
            full((2 * PEER_HEADS, PEER_N_KEYS, PEER_QDIM // 2)),
        ],
        out_specs=[
            row(D_MODEL), row(D_MODEL),
            pl.BlockSpec((2 * PEER_HEADS, PEER_N_KEYS, tm), lambda i: (0, 0, i)),
        ],
        out_shape=[
            jax.ShapeDtypeStruct((T, D_MODEL), F32),
            jax.ShapeDtypeStruct((T, D_MODEL), BF16),
            jax.ShapeDtypeStruct((2 * PEER_HEADS, PEER_N_KEYS, T), F32),
        ],
        compiler_params=_cparams(("parallel",)),
        name="outproj_peer_scores",
    )(ya, yb, yc, x, wa, wb, wc, g, wq_t, sub_keys)


_FRONTIER = [(r, c) for r in range(PEER_TOPK) for c in range(PEER_TOPK) if (r + 1) * (c + 1) <= PEER_TOPK]


def _topk_kernel(s_ref, ia_ref, ib_ref, g_ref, work_ref, val_ref, idx_ref):
    neg_inf = jnp.full((8, LANES), -jnp.inf, F32)
    zero = jnp.zeros((8, LANES), F32)

    def scan(values, ids):
        m, e = neg_inf, zero
        for v, i in zip(values, ids):
            gt = v > m
            m = jnp.maximum(m, v)
            e = jnp.where(gt, i, e)
        return m, e

    def merge(parts):
        m, e = parts[0]
        for m2, e2 in parts[1:]:
            gt = m2 > m
            m = jnp.maximum(m, m2)
            e = jnp.where(gt, e2, e)
        return m, e

    n_chain = 8
    per_chain = PEER_N_KEYS // n_chain
    for p in range(2):
        work_ref[...] = s_ref[p]

        def extract(r, prev, p=p):
            parts = []
            for ch in range(n_chain):
                vals = []
                for k in range(ch * per_chain, (ch + 1) * per_chain):
                    s = jnp.where(prev == float(k), -jnp.inf, work_ref[k])
                    work_ref[k] = s
                    vals.append(s)
                parts.append(scan(vals, [float(k) for k in range(ch * per_chain, (ch + 1) * per_chain)]))
            m, idx = merge(parts)
            val_ref[p, r] = m
            idx_ref[p, r] = idx.astype(I32)
            return idx

        lax.fori_loop(0, PEER_TOPK, extract, jnp.full((8, LANES), -1.0, F32))

    va = [val_ref[0, r] for r in range(PEER_TOPK)]
    vb = [val_ref[1, r] for r in range(PEER_TOPK)]
    ia = [idx_ref[0, r] for r in range(PEER_TOPK)]
    ib = [idx_ref[1, r] for r in range(PEER_TOPK)]
    cand = [va[r] + vb[c] for r, c in _FRONTIER]
    cid = [(ia[r] * PEER_N_KEYS + ib[c]).astype(F32) for r, c in _FRONTIER]
    n_cand = len(_FRONTIER)
    bounds = [n_cand * i // 4 for i in range(5)]
    best = []
    for k in range(PEER_TOPK):
        m, eid_f = merge([scan(cand[lo:hi], cid[lo:hi]) for lo, hi in zip(bounds[:-1], bounds[1:])])
        cand = [jnp.where(ci == eid_f, -jnp.inf, cv) for ci, cv in zip(cid, cand)]
        best.append(m)
        eid = eid_f.astype(I32)
        ia_ref[0, k] = eid >> 7
        ib_ref[0, k] = eid & (PEER_N_KEYS - 1)
    ex = [jnp.exp(b - best[0]) for b in best]
    inv = 1.0 / functools.reduce(jnp.add, ex)
    for k in range(PEER_TOPK):
        g_ref[0, k] = ex[k] * inv


def _topk(scores4):
    nt = scores4.shape[2]
    out_spec = pl.BlockSpec((1, PEER_TOPK, 8, LANES), lambda i, h: (h, 0, i, 0))
    out_sds = lambda dt: jax.ShapeDtypeStruct((PEER_HEADS, PEER_TOPK, nt, LANES), dt)
    return pl.pallas_call(
        _topk_kernel,
        grid=(nt // 8, PEER_HEADS),
        in_specs=[pl.BlockSpec((2, PEER_N_KEYS, 8, LANES), lambda i, h: (h, 0, i, 0))],
        out_specs=[out_spec, out_spec, out_spec],
        out_shape=[out_sds(I32), out_sds(I32), out_sds(F32)],
        scratch_shapes=[
            pltpu.VMEM((PEER_N_KEYS, 8, LANES), F32),
            pltpu.VMEM((2, PEER_TOPK, 8, LANES), F32),
            pltpu.VMEM((2, PEER_TOPK, 8, LANES), I32),
        ],
        compiler_params=_cparams(("parallel", "parallel")),
        name="peer_topk",
    )(scores4)


def _peer_kernel(xn_ref, x1_ref, a_ref, b_ref, g_ref, u_ref, v_ref, o_ref, wt_ref, h_ref, *, tm, te, grp):
    j = pl.program_id(1)
    n_sub = te // PEER_N_KEYS

    @pl.when(j == 0)
    def _():
        o_ref[...] = x1_ref[...]
        key_iota = lax.broadcasted_iota(I32, (PEER_N_KEYS, LANES), 0)

        def build(i, _):
            t0 = pl.multiple_of(i * grp, grp)
            ws = []
            for r in range(grp):
                a_row = a_ref[pl.ds(t0 + r, 1), :]
                b_row = b_ref[pl.ds(t0 + r, 1), :]
                g_row = g_ref[pl.ds(t0 + r, 1), :]
                oa = jnp.where(key_iota == a_row, 1.0, 0.0).astype(BF16)
                gob = jnp.where(key_iota == b_row, g_row, 0.0).astype(BF16)
                ws.append(_dot_nt(oa, gob))
            wt = pltpu.einshape("tij->itj", jnp.stack(ws, axis=0))
            wt_ref[:, pl.ds(t0, grp), :] = wt.astype(BF16)
            return 0

        lax.fori_loop(0, tm // grp, build, 0)

    act = jax.nn.gelu(_dot_nt(xn_ref[...], u_ref[...])).astype(BF16)
    for i in range(n_sub):
        h_ref[:, i * LANES:(i + 1) * LANES] = act[:, i * LANES:(i + 1) * LANES] * wt_ref[j * n_sub + i]
    o_ref[...] += _dot(h_ref[...], v_ref[...])


def _peer(xn, x1, ids_a, ids_b, gates, u_tab, v_tab, tm=512, te=1024, grp=16):
    T = xn.shape[0]
    row = lambda w: pl.BlockSpec((tm, w), lambda i, j: (i, 0))
    return pl.pallas_call(
        functools.partial(_peer_kernel, tm=tm, te=te, grp=grp),
        grid=(T // tm, PEER_N_EXPERTS // te),
        in_specs=[
            row(D_MODEL), row(D_MODEL), row(LANES), row(LANES), row(LANES),
            pl.BlockSpec((te, D_MODEL), lambda i, j: (j, 0)),
            pl.BlockSpec((te, D_MODEL), lambda i, j: (j, 0)),
        ],
        out_specs=row(D_MODEL),
        out_shape=jax.ShapeDtypeStruct((T, D_MODEL), F32),
        scratch_shapes=[
            pltpu.VMEM((PEER_N_KEYS, tm, LANES), BF16),
            pltpu.VMEM((tm, te), BF16),
        ],
        compiler_params=_cparams(("parallel", "arbitrary")),
        name="peer_mix",
    )(xn, x1, ids_a, ids_b, gates, u_tab, v_tab)


def _final_norm_kernel(x_ref, g_ref, o_ref):
    o_ref[...] = _rmsnorm(x_ref[...], g_ref[...])


def _final_norm(x, g, tm=512):
    T = x.shape[0]
    return pl.pallas_call(
        _final_norm_kernel,
        grid=(T // tm,),
        in_specs=[pl.BlockSpec((tm, D_MODEL), lambda i: (i, 0)), pl.BlockSpec((1, D_MODEL), lambda i: (0, 0))],
        out_specs=pl.BlockSpec((tm, D_MODEL), lambda i: (i, 0)),
        out_shape=jax.ShapeDtypeStruct((T, D_MODEL), F32),
        compiler_params=_cparams(("parallel",)),
        name="final_norm",
    )(x, g)


def _prep_w_in(w):
    u, v, cq = w[:, 0:256], w[:, 256:512], w[:, 512:768]
    ckv, kr = w[:, 768:896], w[:, 896:928]
    hq, hf, hi, hg = w[:, 928:1184], w[:, 1184:1440], w[:, 1440:1696], w[:, 1696:1952]
    half = MLA_ROPE // 2
    kr_sw = jnp.concatenate([kr[:, half:], kr[:, :half]], axis=1)
    z64 = jnp.zeros((D_MODEL, MLA_NOPE), w.dtype)
    z32 = jnp.zeros((D_MODEL, LANES - MLA_NOPE - MLA_ROPE), w.dtype)
    kra = jnp.concatenate([z64, kr, z32], axis=1)
    krb = jnp.concatenate([z64, kr_sw, z32], axis=1)
    w_perm = jnp.concatenate([u, v, cq, hq, hf, hi, hg, ckv, kra, krb], axis=1).astype(BF16)
    return w_perm, hi.T.astype(BF16)


def _prep_w_uq(w):
    half = MLA_ROPE // 2
    w3 = w.reshape(MLA_Q_RANK, MLA_HEADS, MLA_NOPE + MLA_ROPE)
    nope, rope = w3[:, :, :MLA_NOPE], w3[:, :, MLA_NOPE:]
    rope_sw = jnp.concatenate([rope[:, :, half:], rope[:, :, :half]], axis=-1)
    z32 = jnp.zeros((MLA_Q_RANK, MLA_HEADS, LANES - MLA_NOPE - MLA_ROPE), w.dtype)
    z64 = jnp.zeros((MLA_Q_RANK, MLA_HEADS, MLA_NOPE), w.dtype)
    main = jnp.concatenate([nope, rope, z32], axis=-1).reshape(MLA_Q_RANK, MLA_HEADS * LANES)
    swap = jnp.concatenate([z64, rope_sw, z32], axis=-1).reshape(MLA_Q_RANK, MLA_HEADS * LANES)
    return jnp.concatenate([main, swap], axis=1).astype(BF16)


def _prep_w_ukv(w):
    w3 = w.reshape(MLA_KV_RANK, MLA_HEADS, MLA_NOPE + MLA_V)
    k_nope, v = w3[:, :, :MLA_NOPE], w3[:, :, MLA_NOPE:]
    z = jnp.zeros((MLA_KV_RANK, MLA_HEADS, LANES - MLA_NOPE), w.dtype)
    wk_pad = jnp.concatenate([k_nope, z], axis=-1).reshape(MLA_KV_RANK, MLA_HEADS * LANES).astype(BF16)
    wv_t = v.reshape(MLA_KV_RANK, MLA_WIDTH).T.astype(BF16)
    return wk_pad, wv_t


def kernel(x, positions, norm_mix, w_in, gm_v_norm, gm_ws, gm_b, gm_out_norm, mla_q_norm, mla_w_uq, mla_kv_norm, mla_w_ukv, mla_out_norm, hg_lb_logits, hg_out_norm, w_out, norm_ffn, peer_w_q, peer_sub_keys, peer_u, peer_v, norm_final):
    batch, seq, _ = x.shape
    T = batch * seq
    depth = w_in.shape[0]
    xt = x.reshape(T, D_MODEL)
    pos_b = jnp.broadcast_to(positions.reshape(T, 1).astype(F32), (T, LANES))
    cos_t, sin_t = _rope_tables(pos_b)
    p = jax.nn.softmax(hg_lb_logits.astype(F32), axis=0)
    lower = jnp.cumsum(p, axis=0) - p[0]
    msum = jnp.asarray(_hgrn_sum_matrix(), BF16)
    row = lambda v: v.reshape(1, -1)

    for l in range(depth):
        w_perm, w_hit = _prep_w_in(w_in[l])
        proj, hit = _inproj(xt, row(norm_mix[l]), w_perm, w_hit)
        bias2d = jnp.repeat(gm_b[l].T, HEAD_DIM, axis=1)
        ya = _gmlp(proj, row(gm_v_norm[l]), gm_ws[l], bias2d, row(gm_out_norm[l]))
        wk_pad, wv_t = _prep_w_ukv(mla_w_ukv[l])
        q, k, vt = _mla_proj(proj, cos_t, sin_t, row(mla_q_norm[l]), row(mla_kv_norm[l]),
                             _prep_w_uq(mla_w_uq[l]), wk_pad, wv_t)
        gain_b = jnp.broadcast_to(mla_out_norm[l][:, None], (MLA_WIDTH, LANES))
        yb = _attention(q, k, vt, gain_b, batch, seq)
        yc = _hgrn(proj, hit, row(lower[l]), row(hg_out_norm[l]), msum, batch, seq)
        wo = w_out[l].astype(BF16)
        sub_keys = peer_sub_keys[l].reshape(2 * PEER_HEADS, PEER_N_KEYS, PEER_QDIM // 2).astype(BF16)
        x1, xn, scores = _outproj(ya, yb, yc, xt, wo[:GM_WIDTH], wo[GM_WIDTH:GM_WIDTH + MLA_WIDTH],
                                  wo[GM_WIDTH + MLA_WIDTH:], row(norm_ffn[l]), peer_w_q[l].T.astype(BF16), sub_keys)
        ia, ib, gt = _topk(scores.reshape(2 * PEER_HEADS, PEER_N_KEYS, T // LANES, LANES))
        to_rows = lambda a: a.reshape(PEER_HEADS * PEER_TOPK, T).T
        xt = _peer(xn, x1, to_rows(ia), to_rows(ib), to_rows(gt),
                   peer_u[l].astype(BF16), peer_v[l].astype(BF16))
    return _final_norm(xt, row(norm_final)).reshape(batch, seq, D_MODEL)
```

```python
import functools
import math

import numpy as np
import jax
import jax.numpy as jnp
from jax import lax
from jax.experimental import pallas as pl
from jax.experimental.pallas import tpu as pltpu

F32 = jnp.float32
BF16 = jnp.bfloat16
I32 = jnp.int32

D_MODEL = 1024
DEPTH = 4
GM_HEADS = 4
GM_WIDTH = 256
GM_CHUNK = 128
MLA_HEADS = 8
MLA_NOPE = 64
MLA_ROPE = 32
MLA_V = 64
MLA_WIDTH = 512
MLA_Q_RANK = 256
MLA_KV_RANK = 128
ROPE_THETA = 10000.0
MASK_VALUE = -1e30
HG_HEADS = 4
HG_WIDTH = 256
GATE_FLOOR = 1e-20
PEER_HEADS = 8
PEER_N_KEYS = 128
PEER_N_EXPERTS = PEER_N_KEYS * PEER_N_KEYS
PEER_TOPK = 16
PEER_QDIM = 128
NORM_EPS = 1e-6

LANES = 128
HEAD_DIM = 64
VMEM_LIMIT = 56 * 1024 * 1024

PROJ_WIDTH = 2176
COL_U, COL_V, COL_CQ, COL_HQ, COL_HF, COL_HI, COL_HG = 0, 256, 512, 768, 1024, 1280, 1536
COL_CKV, COL_KRA, COL_KRB = 1792, 1920, 2048

HG_CHUNK = 128
HG_LEVELS = 7

_NT = (((1,), (1,)), ((), ()))


def _cparams(sem):
    return pltpu.CompilerParams(dimension_semantics=sem, vmem_limit_bytes=VMEM_LIMIT)


def _dot(a, b):
    return jnp.dot(a, b, preferred_element_type=F32)


def _dot_nt(a, b):
    return lax.dot_general(a, b, _NT, preferred_element_type=F32)


def _split2(x):
    hi = x.astype(BF16)
    lo = (x - hi.astype(F32)).astype(BF16)
    return hi, lo


def _group_ones(width):
    r = lax.broadcasted_iota(I32, (width, width), 0) // HEAD_DIM
    c = lax.broadcasted_iota(I32, (width, width), 1) // HEAD_DIM
    return jnp.where(r == c, 1.0, 0.0).astype(BF16)


def _group_sum(x, ones_bd):
    hi, lo = _split2(x)
    return _dot(hi, ones_bd) + _dot(lo, ones_bd)


def _head_rmsnorm(x, gain, ones_bd):
    ms = _group_sum(x * x, ones_bd) * (1.0 / HEAD_DIM)
    return x * lax.rsqrt(ms + NORM_EPS) * gain


def _rmsnorm(x, gain):
    ms = jnp.mean(x * x, axis=-1, keepdims=True)
    return x * lax.rsqrt(ms + NORM_EPS) * gain


def _inproj_kernel(x_ref, g_ref, w_ref, wt_ref, o_ref, ot_ref):
    n = _rmsnorm(x_ref[...], g_ref[...]).astype(BF16)
    o_ref[...] = _dot(n, w_ref[...])
    ot_ref[...] = _dot_nt(wt_ref[...], n)


def _inproj(x, g, w, w_hit, tm=512):
    T = x.shape[0]
    return pl.pallas_call(
        _inproj_kernel,
        grid=(T // tm,),
        in_specs=[
            pl.BlockSpec((tm, D_MODEL), lambda i: (i, 0)),
            pl.BlockSpec((1, D_MODEL), lambda i: (0, 0)),
            pl.BlockSpec((D_MODEL, PROJ_WIDTH), lambda i: (0, 0)),
            pl.BlockSpec((HG_WIDTH, D_MODEL), lambda i: (0, 0)),
        ],
        out_specs=[
            pl.BlockSpec((tm, PROJ_WIDTH), lambda i: (i, 0)),
            pl.BlockSpec((HG_WIDTH, tm), lambda i: (0, i)),
        ],
        out_shape=[
            jax.ShapeDtypeStruct((T, PROJ_WIDTH), F32),
            jax.ShapeDtypeStruct((HG_WIDTH, T), F32),
        ],
        compiler_params=_cparams(("parallel",)),
        name="inproj",
    )(x, g, w, w_hit)


def _rope_table_kernel(pos_ref, c_ref, s_ref):
    lane = lax.broadcasted_iota(I32, (1, LANES), 1)
    half = MLA_ROPE // 2
    jf = ((lane - MLA_NOPE) & (half - 1)).astype(F32)
    inv_freq = jnp.exp(-math.log(ROPE_THETA) * jf / half)
    ang = pos_ref[...] * inv_freq
    c, s = jnp.cos(ang), jnp.sin(ang)
    in_rope = (lane >= MLA_NOPE) & (lane < MLA_NOPE + MLA_ROPE)
    c_ref[...] = jnp.where(lane < MLA_NOPE, 1.0, jnp.where(in_rope, c, 0.0))
    s_ref[...] = jnp.where(in_rope, jnp.where(lane < MLA_NOPE + half, -s, s), 0.0)


def _rope_tables(pos_b, tm=512):
    T = pos_b.shape[0]
    spec = pl.BlockSpec((tm, LANES), lambda i: (i, 0))
    return pl.pallas_call(
        _rope_table_kernel,
        grid=(T // tm,),
        in_specs=[spec],
        out_specs=[spec, spec],
        out_shape=[jax.ShapeDtypeStruct((T, LANES), F32)] * 2,
        compiler_params=_cparams(("parallel",)),
        name="rope_tables",
    )(pos_b)


def _gmlp_kernel(u_ref, v_ref, vg_ref, ws_ref, bias_ref, og_ref, o_ref):
    ones_bd = _group_ones(GM_WIDTH)
    u = jax.nn.gelu(u_ref[...])
    v = _head_rmsnorm(jax.nn.gelu(v_ref[...]), vg_ref[...], ones_bd)
    r = lax.broadcasted_iota(I32, (GM_CHUNK, GM_CHUNK), 0)
    c = lax.broadcasted_iota(I32, (GM_CHUNK, GM_CHUNK), 1)
    causal = c <= r
    lane = lax.broadcasted_iota(I32, (GM_CHUNK, LANES), 1)
    zs = []
    for pair in range(GM_HEADS // 2):
        vp = v[:, pair * LANES:(pair + 1) * LANES].astype(BF16)
        z0 = _dot(jnp.where(causal, ws_ref[2 * pair], 0.0).astype(BF16), vp)
        z1 = _dot(jnp.where(causal, ws_ref[2 * pair + 1], 0.0).astype(BF16), vp)
        zs.append(jnp.where(lane < HEAD_DIM, z0, z1))
    z = jnp.concatenate(zs, axis=1) + bias_ref[...]
    o_ref[...] = _head_rmsnorm(u * z, og_ref[...], ones_bd).astype(BF16)


def _gmlp(proj, v_gain, ws, bias2d, out_gain):
    T = proj.shape[0]
    return pl.pallas_call(
        _gmlp_kernel,
        grid=(T // GM_CHUNK,),
        in_specs=[
            pl.BlockSpec((GM_CHUNK, GM_WIDTH), lambda i: (i, COL_U // GM_WIDTH)),
            pl.BlockSpec((GM_CHUNK, GM_WIDTH), lambda i: (i, COL_V // GM_WIDTH)),
            pl.BlockSpec((1, GM_WIDTH), lambda i: (0, 0)),
            pl.BlockSpec((GM_HEADS, GM_CHUNK, GM_CHUNK), lambda i: (0, 0, 0)),
            pl.BlockSpec((GM_CHUNK, GM_WIDTH), lambda i: (0, 0)),
            pl.BlockSpec((1, GM_WIDTH), lambda i: (0, 0)),
        ],
        out_specs=pl.BlockSpec((GM_CHUNK, GM_WIDTH), lambda i: (i, 0)),
        out_shape=jax.ShapeDtypeStruct((T, GM_WIDTH), BF16),
        compiler_params=_cparams(("parallel",)),
        name="gmlp",
    )(proj, proj, v_gain, ws, bias2d, out_gain)


def _mla_proj_kernel(cq_ref, ckv_ref, kra_ref, krb_ref, c_ref, s_ref, qn_ref, kvn_ref,
                     wq_ref, wk_ref, wvt_ref, q_ref, k_ref, vt_ref):
    cos1, sin1 = c_ref[...], s_ref[...]
    cos8 = jnp.concatenate([cos1] * MLA_HEADS, axis=1)
    sin8 = jnp.concatenate([sin1] * MLA_HEADS, axis=1)
    nq = _rmsnorm(cq_ref[...], qn_ref[...]).astype(BF16)
    qq = _dot(nq, wq_ref[...])
    width = MLA_HEADS * LANES
    scale = (MLA_NOPE + MLA_ROPE) ** -0.5
    q_ref[...] = ((qq[:, :width] * cos8 + qq[:, width:] * sin8) * scale).astype(BF16)
    nkv = _rmsnorm(ckv_ref[...], kvn_ref[...]).astype(BF16)
    kr = kra_ref[...] * cos1 + krb_ref[...] * sin1
    k_ref[...] = (_dot(nkv, wk_ref[...]) + jnp.concatenate([kr] * MLA_HEADS, axis=1)).astype(BF16)
    vt_ref[...] = _dot_nt(wvt_ref[...], nkv).astype(BF16)


def _mla_proj(proj, cos_t, sin_t, q_norm, kv_norm, wq_cat, wk_pad, wv_t, tm=512):
    T = proj.shape[0]
    width = MLA_HEADS * LANES
    full = lambda shape: pl.BlockSpec(shape, lambda i: (0,) * len(shape))
    return pl.pallas_call(
        _mla_proj_kernel,
        grid=(T // tm,),
        in_specs=[
            pl.BlockSpec((tm, MLA_Q_RANK), lambda i: (i, COL_CQ // MLA_Q_RANK)),
            pl.BlockSpec((tm, LANES), lambda i: (i, COL_CKV // LANES)),
            pl.BlockSpec((tm, LANES), lambda i: (i, COL_KRA // LANES)),
            pl.BlockSpec((tm, LANES), lambda i: (i, COL_KRB // LANES)),
            pl.BlockSpec((tm, LANES), lambda i: (i, 0)),
            pl.BlockSpec((tm, LANES), lambda i: (i, 0)),
            full((1, MLA_Q_RANK)),
            full((1, MLA_KV_RANK)),
            full((MLA_Q_RANK, 2 * width)),
            full((MLA_KV_RANK, width)),
            full((MLA_WIDTH, MLA_KV_RANK)),
        ],
        out_specs=[
            pl.BlockSpec((tm, width), lambda i: (i, 0)),
            pl.BlockSpec((tm, width), lambda i: (i, 0)),
            pl.BlockSpec((MLA_WIDTH, tm), lambda i: (0, i)),
        ],
        out_shape=[
            jax.ShapeDtypeStruct((T, width), BF16),
            jax.ShapeDtypeStruct((T, width), BF16),
            jax.ShapeDtypeStruct((MLA_WIDTH, T), BF16),
        ],
        compiler_params=_cparams(("parallel",)),
        name="mla_proj",
    )(proj, proj, proj, proj, cos_t, sin_t, q_norm, kv_norm, wq_cat, wk_pad, wv_t)


def _attn_kernel(q_ref, k_ref, vt_ref, g_ref, o_ref, *, blk):
    qi = pl.program_id(2)
    kpos = lax.broadcasted_iota(I32, (blk, blk), 0)
    qpos = lax.broadcasted_iota(I32, (blk, blk), 1)
    gain = jnp.concatenate([g_ref[...]] * (blk // LANES), axis=1)
    qs = [q_ref[:, h * LANES:(h + 1) * LANES] for h in range(2)]

    def step(j, carry, masked):
        off = pl.multiple_of(j * blk, blk)
        new = []
        for h in range(2):
            m, l, acc = carry[3 * h:3 * h + 3]
            kb = k_ref[pl.ds(off, blk), h * LANES:(h + 1) * LANES]
            st = _dot_nt(kb, qs[h])
            if masked:
                st = jnp.where(kpos <= qpos, st, MASK_VALUE)
            m_new = jnp.maximum(m, jnp.max(st, axis=0, keepdims=True))
            p = jnp.exp(st - m_new)
            alpha = jnp.exp(m - m_new)
            l = alpha * l + jnp.sum(p, axis=0, keepdims=True)
            vt = vt_ref[h * MLA_V:(h + 1) * MLA_V, pl.ds(off, blk)]
            acc = alpha * acc + _dot(vt, p.astype(BF16))
            new += [m_new, l, acc]
        return tuple(new)

    init = (jnp.full((1, blk), -jnp.inf, F32), jnp.zeros((1, blk), F32), jnp.zeros((MLA_V, blk), F32)) * 2
    carry = lax.fori_loop(0, qi, functools.partial(step, masked=False), init)
    carry = step(qi, carry, True)
    outs = []
    for h in range(2):
        _, l, acc = carry[3 * h:3 * h + 3]
        o = acc / l
        ms = jnp.mean(o * o, axis=0, keepdims=True)
        outs.append(o * lax.rsqrt(ms + NORM_EPS) * gain[h * MLA_V:(h + 1) * MLA_V])
    o_ref[...] = jnp.concatenate(outs, axis=0).T.astype(BF16)


def _attention(q, k, vt, gain_b, batch, seq, blk=512):
    T = q.shape[0]
    nq = seq // blk
    return pl.pallas_call(
        functools.partial(_attn_kernel, blk=blk),
        grid=(batch, MLA_HEADS // 2, nq),
        in_specs=[
            pl.BlockSpec((blk, 2 * LANES), lambda b, hp, i: (b * nq + i, hp)),
            pl.BlockSpec((seq, 2 * LANES), lambda b, hp, i: (b, hp)),
            pl.BlockSpec((2 * MLA_V, seq), lambda b, hp, i: (hp, b)),
            pl.BlockSpec((2 * MLA_V, LANES), lambda b, hp, i: (hp, 0)),
        ],
        out_specs=pl.BlockSpec((blk, 2 * MLA_V), lambda b, hp, i: (b * nq + i, hp)),
        out_shape=jax.ShapeDtypeStruct((T, MLA_WIDTH), BF16),
        compiler_params=_cparams(("parallel", "parallel", "arbitrary")),
        name="mla_attention",
    )(q, k, vt, gain_b)


def _hgrn_sum_matrix():
    c = HG_CHUNK
    t = np.arange(c)[:, None]
    j = np.arange(c)[None, :]
    blocks = [(j <= t), (j > t)]
    for lv in range(HG_LEVELS):
        m = c >> (lv + 1)
        mid = (t // (2 * m)) * 2 * m + m - 1
        right = t > mid
        blocks.append(np.where(right, (j > mid) & (j <= t), (j > t) & (j <= mid)))
    mat = np.concatenate(blocks, axis=0).astype(np.float32)
    return np.concatenate([mat, mat, mat], axis=1)


def _hgrn_kernel(hq_ref, hf_ref, hi_ref, hit_ref, hg_ref, lb_ref, gain_ref, msum_ref, o_ref, st_ref, *, cb):
    c = HG_CHUNK

    @pl.when(pl.program_id(1) == 0)
    def _():
        st_ref[...] = jnp.zeros_like(st_ref)

    ones_bd = _group_ones(HG_WIDTH)
    lb = lb_ref[...]
    t_row = lax.broadcasted_iota(I32, (c, HG_WIDTH), 0)
    tt = lax.broadcasted_iota(I32, (c, c), 0)
    ss = lax.broadcasted_iota(I32, (c, c), 1)
    lane = lax.broadcasted_iota(I32, (c, LANES), 1)
    bd_mask = (tt // HEAD_DIM) == (ss // HEAD_DIM)
    level_masks = []
    for lv in range(HG_LEVELS):
        m = c >> (lv + 1)
        same_block = (tt // (2 * m)) == (ss // (2 * m))
        level_masks.append(same_block & ((tt & m) != 0) & ((ss & m) == 0))

    for ch in range(cb // c):
        rows = slice(ch * c, (ch + 1) * c)
        fx = hf_ref[rows, :]
        f_gate = lb + (1.0 - lb) * jax.nn.sigmoid(fx)
        lf = jnp.log(jnp.maximum(f_gate, GATE_FLOOR))
        kf = (1.0 - lb) * jax.nn.sigmoid(-fx)
        qf = jax.nn.silu(hq_ref[rows, :])
        vf = hi_ref[rows, :]
        lf_hi = lf.astype(BF16)
        r1 = lf - lf_hi.astype(F32)
        lf_mid = r1.astype(BF16)
        lf_lo = (r1 - lf_mid.astype(F32)).astype(BF16)
        e = jnp.exp(_dot(msum_ref[...], jnp.concatenate([lf_hi, lf_mid, lf_lo], axis=0)))
        eb = e[0:c]
        es = e[c:2 * c]
        q_in = (qf * eb).astype(BF16)
        k_out = (kf * es).astype(BF16)
        ys = []
        for lv in range(HG_LEVELS):
            m = c >> (lv + 1)
            ys.append(jnp.where((t_row & m) != 0, qf, kf) * e[(2 + lv) * c:(3 + lv) * c])
        o_parts = []
        for pair in range(HG_HEADS // 2):
            lanes = slice(pair * LANES, (pair + 1) * LANES)
            state_t = st_ref[pair]
            o_inter = _dot_nt(q_in[:, lanes], state_t.astype(BF16))
            a = [jnp.zeros((c, c), F32), jnp.zeros((c, c), F32)]
            for lv in range(HG_LEVELS):
                y = ys[lv][:, lanes]
                yb = y.astype(BF16)
                for h in range(2):
                    yh = jnp.where((lane // HEAD_DIM) == h, y, 0.0).astype(BF16)
                    a[h] = a[h] + jnp.where(level_masks[lv], _dot_nt(yh, yb), 0.0)
            vp = vf[:, lanes].astype(BF16)
            o_intra = jnp.where(lane < HEAD_DIM, _dot(a[0].astype(BF16), vp), _dot(a[1].astype(BF16), vp))
            o_parts.append(o_inter + o_intra)
            upd = _dot(hit_ref[lanes, rows].astype(BF16), k_out[:, lanes])
            st_ref[pair] = state_t * eb[c - 1:c, lanes] + jnp.where(bd_mask, upd, 0.0)
        o = jnp.concatenate(o_parts, axis=1) + _group_sum(qf * kf, ones_bd) * vf
        y_out = _head_rmsnorm(o, gain_ref[...], ones_bd) * jax.nn.silu(hg_ref[rows, :])
        o_ref[rows, :] = y_out.astype(BF16)


def _hgrn(proj, hit, lb, gain, msum, batch, seq, cb=512):
    T = proj.shape[0]
    nb = seq // cb
    col = lambda off: pl.BlockSpec((cb, HG_WIDTH), lambda b, i: (b * nb + i, off // HG_WIDTH))
    return pl.pallas_call(
        functools.partial(_hgrn_kernel, cb=cb),
        grid=(batch, nb),
        in_specs=[
            col(COL_HQ), col(COL_HF), col(COL_HI),
            pl.BlockSpec((HG_WIDTH, cb), lambda b, i: (0, b * nb + i)),
            col(COL_HG),
            pl.BlockSpec((1, HG_WIDTH), lambda b, i: (0, 0)),
            pl.BlockSpec((1, HG_WIDTH), lambda b, i: (0, 0)),
            pl.BlockSpec(msum.shape, lambda b, i: (0, 0)),
        ],
        out_specs=pl.BlockSpec((cb, HG_WIDTH), lambda b, i: (b * nb + i, 0)),
        out_shape=jax.ShapeDtypeStruct((T, HG_WIDTH), BF16),
        scratch_shapes=[pltpu.VMEM((HG_HEADS // 2, LANES, LANES), F32)],
        compiler_params=_cparams(("parallel", "arbitrary")),
        name="hgrn2",
    )(proj, proj, proj, hit, proj, lb, gain, msum)


def _outproj_kernel(ya_ref, yb_ref, yc_ref, x_ref, wa_ref, wb_ref, wc_ref, g_ref, wqt_ref, sk_ref,
                    x1_ref, xn_ref, s_ref):
    x1 = x_ref[...] + _dot(ya_ref[...], wa_ref[...]) + _dot(yb_ref[...], wb_ref[...]) + _dot(yc_ref[...], wc_ref[...])
    x1_ref[...] = x1
    xn = _rmsnorm(x1, g_ref[...]).astype(BF16)
    xn_ref[...] = xn
    qt = _dot_nt(wqt_ref[...], xn)
    half = PEER_QDIM // 2
    for hp in range(2 * PEER_HEADS):
        s_ref[hp] = _dot(sk_ref[hp], qt[hp * half:(hp + 1) * half].astype(BF16))


def _outproj(ya, yb, yc, x, wa, wb, wc, g, wq_t, sub_keys, tm=512):
    T = x.shape[0]
    full = lambda shape: pl.BlockSpec(shape, lambda i: (0,) * len(shape))
    row = lambda w: pl.BlockSpec((tm, w), lambda i: (i, 0))
    return pl.pallas_call(
        _outproj_kernel,
        grid=(T // tm,),
        in_specs=[
            row(GM_WIDTH), row(MLA_WIDTH), row(HG_WIDTH), row(D_MODEL),
            full((GM_WIDTH, D_MODEL)), full((MLA_WIDTH, D_MODEL)), full((HG_WIDTH, D_MODEL)),
            full((1, D_MODEL)), full((D_MODEL, D_MODEL)),
            full((2 * PEER_HEADS, PEER_N_KEYS, PEER_QDIM // 2)),
        ],
        out_specs=[
            row(D_MODEL), row(D_MODEL),
            pl.BlockSpec((2 * PEER_HEADS, PEER_N_KEYS, tm), lambda i: (0, 0, i)),
        ],
        out_shape=[
            jax.ShapeDtypeStruct((T, D_MODEL), F32),
            jax.ShapeDtypeStruct((T, D_MODEL), BF16),
            jax.ShapeDtypeStruct((2 * PEER_HEADS, PEER_N_KEYS, T), F32),
        ],
        compiler_params=_cparams(("parallel",)),
        name="outproj_peer_scores",
    )(ya, yb, yc, x, wa, wb, wc, g, wq_t, sub_keys)


_FRONTIER = [(r, c) for r in range(PEER_TOPK) for c in range(PEER_TOPK) if (r + 1) * (c + 1) <= PEER_TOPK]


def _topk_kernel(s_ref, ia_ref, ib_ref, g_ref, work_ref, val_ref, idx_ref):
    neg_inf = jnp.full((8, LANES), -jnp.inf, F32)
    zero = jnp.zeros((8, LANES), F32)

    def scan(values, ids):
        m, e = neg_inf, zero
        for v, i in zip(values, ids):
            gt = v > m
            m = jnp.maximum(m, v)
            e = jnp.where(gt, i, e)
        return m, e

    def merge(parts):
        m, e = parts[0]
        for m2, e2 in parts[1:]:
            gt = m2 > m
            m = jnp.maximum(m, m2)
            e = jnp.where(gt, e2, e)
        return m, e

    n_chain = 8
    per_chain = PEER_N_KEYS // n_chain
    for p in range(2):
        work_ref[...] = s_ref[p]

        def extract(r, prev, p=p):
            parts = []
            for ch in range(n_chain):
                vals = []
                for k in range(ch * per_chain, (ch + 1) * per_chain):
                    s = jnp.where(prev == float(k), -jnp.inf, work_ref[k])
                    work_ref[k] = s
                    vals.append(s)
                parts.append(scan(vals, [float(k) for k in range(ch * per_chain, (ch + 1) * per_chain)]))
            m, idx = merge(parts)
            val_ref[p, r] = m
            idx_ref[p, r] = idx.astype(I32)
            return idx

        lax.fori_loop(0, PEER_TOPK, extract, jnp.full((8, LANES), -1.0, F32))

    va = [val_ref[0, r] for r in range(PEER_TOPK)]
    vb = [val_ref[1, r] for r in range(PEER_TOPK)]
    ia = [idx_ref[0, r] for r in range(PEER_TOPK)]
    ib = [idx_ref[1, r] for r in range(PEER_TOPK)]
    cand = [va[r] + vb[c] for r, c in _FRONTIER]
    cid = [(ia[r] * PEER_N_KEYS + ib[c]).astype(F32) for r, c in _FRONTIER]
    n_cand = len(_FRONTIER)
    bounds = [n_cand * i // 4 for i in range(5)]
    best = []
    for k in range(PEER_TOPK):
        m, eid_f = merge([scan(cand[lo:hi], cid[lo:hi]) for lo, hi in zip(bounds[:-1], bounds[1:])])
        cand = [jnp.where(ci == eid_f, -jnp.inf, cv) for ci, cv in zip(cid, cand)]
        best.append(m)
        eid = eid_f.astype(I32)
        ia_ref[0, k] = eid >> 7
        ib_ref[0, k] = eid & (PEER_N_KEYS - 1)
    ex = [jnp.exp(b - best[0]) for b in best]
    inv = 1.0 / functools.reduce(jnp.add, ex)
    for k in range(PEER_TOPK):
        g_ref[0, k] = ex[k] * inv


def _topk(scores4):
    nt = scores4.shape[2]
    out_spec = pl.BlockSpec((1, PEER_TOPK, 8, LANES), lambda i, h: (h, 0, i, 0))
    out_sds = lambda dt: jax.ShapeDtypeStruct((PEER_HEADS, PEER_TOPK, nt, LANES), dt)
    return pl.pallas_call(
        _topk_kernel,
        grid=(nt // 8, PEER_HEADS),
        in_specs=[pl.BlockSpec((2, PEER_N_KEYS, 8, LANES), lambda i, h: (h, 0, i, 0))],
        out_specs=[out_spec, out_spec, out_spec],
        out_shape=[out_sds(I32), out_sds(I32), out_sds(F32)],
        scratch_shapes=[
            pltpu.VMEM((PEER_N_KEYS, 8, LANES), F32),
            pltpu.VMEM((2, PEER_TOPK, 8, LANES), F32),
            pltpu.VMEM((2, PEER_TOPK, 8, LANES), I32),
        ],
        compiler_params=_cparams(("parallel", "parallel")),
        name="peer_topk",
    )(scores4)


def _peer_kernel(xn_ref, x1_ref, a_ref, b_ref, g_ref, ut_ref, v_ref, o_ref, wt_ref, h_ref, at_ref, bt_ref, gt_ref,
                 *, tm, te, grp):
    j = pl.program_id(1)
    n_sub = te // PEER_N_KEYS

    @pl.when(j == 0)
    def _():
        o_ref[...] = x1_ref[...]
        at_ref[...] = a_ref[...].astype(F32).T
        bt_ref[...] = b_ref[...].astype(F32).T
        gt_ref[...] = g_ref[...].T
        key_iota = lax.broadcasted_iota(I32, (PEER_N_KEYS, LANES), 0).astype(F32).astype(BF16)
        one = jnp.ones((PEER_N_KEYS, LANES), BF16)
        zero = jnp.zeros((PEER_N_KEYS, LANES), BF16)

        def build(i, _):
            t0 = pl.multiple_of(i * grp, grp)
            ws = []
            for r in range(grp):
                a_row = at_ref[pl.ds(t0 + r, 1), :].astype(BF16)
                b_row = bt_ref[pl.ds(t0 + r, 1), :].astype(BF16)
                g_row = gt_ref[pl.ds(t0 + r, 1), :].astype(BF16)
                oa = jnp.where(key_iota == a_row, one, zero)
                gob = jnp.where(key_iota == b_row, jnp.broadcast_to(g_row, one.shape), zero)
                ws.append(_dot_nt(oa, gob))
            wt = pltpu.einshape("tij->itj", jnp.stack(ws, axis=0))
            wt_ref[:, pl.ds(t0, grp), :] = wt.astype(BF16)
            return 0

        lax.fori_loop(0, tm // grp, build, 0)

    act = jax.nn.gelu(_dot(xn_ref[...], ut_ref[...])).astype(BF16)
    for i in range(n_sub):
        h_ref[:, i * LANES:(i + 1) * LANES] = act[:, i * LANES:(i + 1) * LANES] * wt_ref[j * n_sub + i]
    o_ref[...] += _dot(h_ref[...], v_ref[...])


def _peer(xn, x1, ids_a, ids_b, gates, u_t, v_tab, tm=512, te=2048, grp=16):
    T = xn.shape[0]
    n_sel = PEER_HEADS * PEER_TOPK
    row = lambda w: pl.BlockSpec((tm, w), lambda i, j: (i, 0))
    sel = pl.BlockSpec((n_sel, tm), lambda i, j: (0, i))
    return pl.pallas_call(
        functools.partial(_peer_kernel, tm=tm, te=te, grp=grp),
        grid=(T // tm, PEER_N_EXPERTS // te),
        in_specs=[
            row(D_MODEL), row(D_MODEL), sel, sel, sel,
            pl.BlockSpec((D_MODEL, te), lambda i, j: (0, j)),
            pl.BlockSpec((te, D_MODEL), lambda i, j: (j, 0)),
        ],
        out_specs=row(D_MODEL),
        out_shape=jax.ShapeDtypeStruct((T, D_MODEL), F32),
        scratch_shapes=[
            pltpu.VMEM((PEER_N_KEYS, tm, LANES), BF16),
            pltpu.VMEM((tm, te), BF16),
            pltpu.VMEM((tm, n_sel), F32),
            pltpu.VMEM((tm, n_sel), F32),
            pltpu.VMEM((tm, n_sel), F32),
        ],
        compiler_params=_cparams(("parallel", "arbitrary")),
        name="peer_mix",
    )(xn, x1, ids_a, ids_b, gates, u_t, v_tab)


def _final_norm_kernel(x_ref, g_ref, o_ref):
    o_ref[...] = _rmsnorm(x_ref[...], g_ref[...])


def _final_norm(x, g, tm=512):
    T = x.shape[0]
    return pl.pallas_call(
        _final_norm_kernel,
        grid=(T // tm,),
        in_specs=[pl.BlockSpec((tm, D_MODEL), lambda i: (i, 0)), pl.BlockSpec((1, D_MODEL), lambda i: (0, 0))],
        out_specs=pl.BlockSpec((tm, D_MODEL), lambda i: (i, 0)),
        out_shape=jax.ShapeDtypeStruct((T, D_MODEL), F32),
        compiler_params=_cparams(("parallel",)),
        name="final_norm",
    )(x, g)


def _prep_w_in(w):
    u, v, cq = w[:, 0:256], w[:, 256:512], w[:, 512:768]
    ckv, kr = w[:, 768:896], w[:, 896:928]
    hq, hf, hi, hg = w[:, 928:1184], w[:, 1184:1440], w[:, 1440:1696], w[:, 1696:1952]
    half = MLA_ROPE // 2
    kr_sw = jnp.concatenate([kr[:, half:], kr[:, :half]], axis=1)
    z64 = jnp.zeros((D_MODEL, MLA_NOPE), w.dtype)
    z32 = jnp.zeros((D_MODEL, LANES - MLA_NOPE - MLA_ROPE), w.dtype)
    kra = jnp.concatenate([z64, kr, z32], axis=1)
    krb = jnp.concatenate([z64, kr_sw, z32], axis=1)
    w_perm = jnp.concatenate([u, v, cq, hq, hf, hi, hg, ckv, kra, krb], axis=1).astype(BF16)
    return w_perm, hi.T.astype(BF16)


def _prep_w_uq(w):
    half = MLA_ROPE // 2
    w3 = w.reshape(MLA_Q_RANK, MLA_HEADS, MLA_NOPE + MLA_ROPE)
    nope, rope = w3[:, :, :MLA_NOPE], w3[:, :, MLA_NOPE:]
    rope_sw = jnp.concatenate([rope[:, :, half:], rope[:, :, :half]], axis=-1)
    z32 = jnp.zeros((MLA_Q_RANK, MLA_HEADS, LANES - MLA_NOPE - MLA_ROPE), w.dtype)
    z64 = jnp.zeros((MLA_Q_RANK, MLA_HEADS, MLA_NOPE), w.dtype)
    main = jnp.concatenate([nope, rope, z32], axis=-1).reshape(MLA_Q_RANK, MLA_HEADS * LANES)
    swap = jnp.concatenate([z64, rope_sw, z32], axis=-1).reshape(MLA_Q_RANK, MLA_HEADS * LANES)
    return jnp.concatenate([main, swap], axis=1).astype(BF16)


def _prep_w_ukv(w):
    w3 = w.reshape(MLA_KV_RANK, MLA_HEADS, MLA_NOPE + MLA_V)
    k_nope, v = w3[:, :, :MLA_NOPE], w3[:, :, MLA_NOPE:]
    z = jnp.zeros((MLA_KV_RANK, MLA_HEADS, LANES - MLA_NOPE), w.dtype)
    wk_pad = jnp.concatenate([k_nope, z], axis=-1).reshape(MLA_KV_RANK, MLA_HEADS * LANES).astype(BF16)
    wv_t = v.reshape(MLA_KV_RANK, MLA_WIDTH).T.astype(BF16)
    return wk_pad, wv_t


def kernel(x, positions, norm_mix, w_in, gm_v_norm, gm_ws, gm_b, gm_out_norm, mla_q_norm, mla_w_uq, mla_kv_norm, mla_w_ukv, mla_out_norm, hg_lb_logits, hg_out_norm, w_out, norm_ffn, peer_w_q, peer_sub_keys, peer_u, peer_v, norm_final):
    batch, seq, _ = x.shape
    T = batch * seq
    depth = w_in.shape[0]
    xt = x.reshape(T, D_MODEL)
    pos_b = jnp.broadcast_to(positions.reshape(T, 1).astype(F32), (T, LANES))
    cos_t, sin_t = _rope_tables(pos_b)
    p = jax.nn.softmax(hg_lb_logits.astype(F32), axis=0)
    lower = jnp.cumsum(p, axis=0) - p[0]
    msum = jnp.asarray(_hgrn_sum_matrix(), BF16)
    row = lambda v: v.reshape(1, -1)

    for l in range(depth):
        w_perm, w_hit = _prep_w_in(w_in[l])
        proj, hit = _inproj(xt, row(norm_mix[l]), w_perm, w_hit)
        bias2d = jnp.repeat(gm_b[l].T, HEAD_DIM, axis=1)
        ya = _gmlp(proj, row(gm_v_norm[l]), gm_ws[l], bias2d, row(gm_out_norm[l]))
        wk_pad, wv_t = _prep_w_ukv(mla_w_ukv[l])
        q, k, vt = _mla_proj(proj, cos_t, sin_t, row(mla_q_norm[l]), row(mla_kv_norm[l]),
                             _prep_w_uq(mla_w_uq[l]), wk_pad, wv_t)
        gain_b = jnp.broadcast_to(mla_out_norm[l][:, None], (MLA_WIDTH, LANES))
        yb = _attention(q, k, vt, gain_b, batch, seq)
        yc = _hgrn(proj, hit, row(lower[l]), row(hg_out_norm[l]), msum, batch, seq)
        wo = w_out[l].astype(BF16)
        sub_keys = peer_sub_keys[l].reshape(2 * PEER_HEADS, PEER_N_KEYS, PEER_QDIM // 2).astype(BF16)
        x1, xn, scores = _outproj(ya, yb, yc, xt, wo[:GM_WIDTH], wo[GM_WIDTH:GM_WIDTH + MLA_WIDTH],
                                  wo[GM_WIDTH + MLA_WIDTH:], row(norm_ffn[l]), peer_w_q[l].T.astype(BF16), sub_keys)
        ia, ib, gt = _topk(scores.reshape(2 * PEER_HEADS, PEER_N_KEYS, T // LANES, LANES))
        flat = lambda a: a.reshape(PEER_HEADS * PEER_TOPK, T)
        xt = _peer(xn, x1, flat(ia), flat(ib), flat(gt),
                   peer_u[l].T.astype(BF16), peer_v[l].astype(BF16))
    return _final_norm(xt, row(norm_final)).reshape(batch, seq, D_MODEL)
```

```python
import functools
import math

import numpy as np
import jax
import jax.numpy as jnp
from jax import lax
from jax.experimental import pallas as pl
from jax.experimental.pallas import tpu as pltpu

F32 = jnp.float32
BF16 = jnp.bfloat16
I32 = jnp.int32
U32 = jnp.uint32
HI16 = np.uint32(0xFFFF0000)

D_MODEL = 1024
DEPTH = 4
GM_HEADS = 4
GM_WIDTH = 256
GM_CHUNK = 128
MLA_HEADS = 8
MLA_NOPE = 64
MLA_ROPE = 32
MLA_V = 64
MLA_WIDTH = 512
MLA_Q_RANK = 256
MLA_KV_RANK = 128
ROPE_THETA = 10000.0
MASK_VALUE = -1e30
HG_HEADS = 4
HG_WIDTH = 256
GATE_FLOOR = 1e-20
PEER_HEADS = 8
PEER_N_KEYS = 128
PEER_N_EXPERTS = PEER_N_KEYS * PEER_N_KEYS
PEER_TOPK = 16
PEER_QDIM = 128
NORM_EPS = 1e-6

LANES = 128
HEAD_DIM = 64
VMEM_LIMIT = 56 * 1024 * 1024

PROJ_WIDTH = 2176
COL_U, COL_V, COL_CQ, COL_HQ, COL_HF, COL_HI, COL_HG = 0, 256, 512, 768, 1024, 1280, 1536
COL_CKV, COL_KRA, COL_KRB = 1792, 1920, 2048

W_PITCH = 72
HG_CHUNK = 128
HG_LEVELS = 7

_NT = (((1,), (1,)), ((), ()))


def _cparams(sem):
    return pltpu.CompilerParams(dimension_semantics=sem, vmem_limit_bytes=VMEM_LIMIT)


def _dot(a, b):
    return jnp.dot(a, b, preferred_element_type=F32)


def _dot_nt(a, b):
    return lax.dot_general(a, b, _NT, preferred_element_type=F32)


def _split2(x):
    hi = x.astype(BF16)
    lo = (x - hi.astype(F32)).astype(BF16)
    return hi, lo


def _group_ones(width):
    r = lax.broadcasted_iota(I32, (width, width), 0) // HEAD_DIM
    c = lax.broadcasted_iota(I32, (width, width), 1) // HEAD_DIM
    return jnp.where(r == c, 1.0, 0.0).astype(BF16)


def _group_sum(x, ones_bd):
    hi, lo = _split2(x)
    return _dot(hi, ones_bd) + _dot(lo, ones_bd)


def _head_rmsnorm(x, gain, ones_bd):
    ms = _group_sum(x * x, ones_bd) * (1.0 / HEAD_DIM)
    return x * lax.rsqrt(ms + NORM_EPS) * gain


def _rmsnorm(x, gain):
    ms = jnp.mean(x * x, axis=-1, keepdims=True)
    return x * lax.rsqrt(ms + NORM_EPS) * gain


def _inproj_kernel(x_ref, g_ref, w_ref, wt_ref, o_ref, ot_ref):
    n = _rmsnorm(x_ref[...], g_ref[...]).astype(BF16)
    o_ref[...] = _dot(n, w_ref[...])
    ot_ref[...] = _dot_nt(wt_ref[...], n)


def _inproj(x, g, w, w_hit, tm=512):
    T = x.shape[0]
    return pl.pallas_call(
        _inproj_kernel,
        grid=(T // tm,),
        in_specs=[
            pl.BlockSpec((tm, D_MODEL), lambda i: (i, 0)),
            pl.BlockSpec((1, D_MODEL), lambda i: (0, 0)),
            pl.BlockSpec((D_MODEL, PROJ_WIDTH), lambda i: (0, 0)),
            pl.BlockSpec((HG_WIDTH, D_MODEL), lambda i: (0, 0)),
        ],
        out_specs=[
            pl.BlockSpec((tm, PROJ_WIDTH), lambda i: (i, 0)),
            pl.BlockSpec((HG_WIDTH, tm), lambda i: (0, i)),
        ],
        out_shape=[
            jax.ShapeDtypeStruct((T, PROJ_WIDTH), F32),
            jax.ShapeDtypeStruct((HG_WIDTH, T), F32),
        ],
        compiler_params=_cparams(("parallel",)),
        name="inproj",
    )(x, g, w, w_hit)


def _rope_table_kernel(pos_ref, c_ref, s_ref):
    lane = lax.broadcasted_iota(I32, (1, LANES), 1)
    half = MLA_ROPE // 2
    jf = ((lane - MLA_NOPE) & (half - 1)).astype(F32)
    inv_freq = jnp.exp(-math.log(ROPE_THETA) * jf / half)
    ang = pos_ref[...] * inv_freq
    c, s = jnp.cos(ang), jnp.sin(ang)
    in_rope = (lane >= MLA_NOPE) & (lane < MLA_NOPE + MLA_ROPE)
    c_ref[...] = jnp.where(lane < MLA_NOPE, 1.0, jnp.where(in_rope, c, 0.0))
    s_ref[...] = jnp.where(in_rope, jnp.where(lane < MLA_NOPE + half, -s, s), 0.0)


def _rope_tables(pos_b, tm=512):
    T = pos_b.shape[0]
    spec = pl.BlockSpec((tm, LANES), lambda i: (i, 0))
    return pl.pallas_call(
        _rope_table_kernel,
        grid=(T // tm,),
        in_specs=[spec],
        out_specs=[spec, spec],
        out_shape=[jax.ShapeDtypeStruct((T, LANES), F32)] * 2,
        compiler_params=_cparams(("parallel",)),
        name="rope_tables",
    )(pos_b)


def _gmlp_kernel(u_ref, v_ref, vg_ref, ws_ref, bias_ref, og_ref, o_ref, *, tb):
    ones_bd = _group_ones(GM_WIDTH)
    r = lax.broadcasted_iota(I32, (GM_CHUNK, GM_CHUNK), 0)
    c = lax.broadcasted_iota(I32, (GM_CHUNK, GM_CHUNK), 1)
    causal = c <= r
    lane = lax.broadcasted_iota(I32, (GM_CHUNK, LANES), 1)
    ws = [jnp.where(causal, ws_ref[h], 0.0).astype(BF16) for h in range(GM_HEADS)]
    for ch in range(tb // GM_CHUNK):
        rows = slice(ch * GM_CHUNK, (ch + 1) * GM_CHUNK)
        u = jax.nn.gelu(u_ref[rows, :])
        v = _head_rmsnorm(jax.nn.gelu(v_ref[rows, :]), vg_ref[...], ones_bd)
        zs = []
        for pair in range(GM_HEADS // 2):
            vp = v[:, pair * LANES:(pair + 1) * LANES].astype(BF16)
            zs.append(jnp.where(lane < HEAD_DIM, _dot(ws[2 * pair], vp), _dot(ws[2 * pair + 1], vp)))
        z = jnp.concatenate(zs, axis=1) + bias_ref[...]
        o_ref[rows, :] = _head_rmsnorm(u * z, og_ref[...], ones_bd).astype(BF16)


def _gmlp(proj, v_gain, ws, bias2d, out_gain, tb=512):
    T = proj.shape[0]
    return pl.pallas_call(
        functools.partial(_gmlp_kernel, tb=tb),
        grid=(T // tb,),
        in_specs=[
            pl.BlockSpec((tb, GM_WIDTH), lambda i: (i, COL_U // GM_WIDTH)),
            pl.BlockSpec((tb, GM_WIDTH), lambda i: (i, COL_V // GM_WIDTH)),
            pl.BlockSpec((1, GM_WIDTH), lambda i: (0, 0)),
            pl.BlockSpec((GM_HEADS, GM_CHUNK, GM_CHUNK), lambda i: (0, 0, 0)),
            pl.BlockSpec((GM_CHUNK, GM_WIDTH), lambda i: (0, 0)),
            pl.BlockSpec((1, GM_WIDTH), lambda i: (0, 0)),
        ],
        out_specs=pl.BlockSpec((tb, GM_WIDTH), lambda i: (i, 0)),
        out_shape=jax.ShapeDtypeStruct((T, GM_WIDTH), BF16),
        compiler_params=_cparams(("parallel",)),
        name="gmlp",
    )(proj, proj, v_gain, ws, bias2d, out_gain)


def _mla_proj_kernel(cq_ref, ckv_ref, kra_ref, krb_ref, c_ref, s_ref, qn_ref, kvn_ref,
                     wq_ref, wk_ref, wvt_ref, q_ref, k_ref, vt_ref):
    cos1, sin1 = c_ref[...], s_ref[...]
    cos8 = jnp.concatenate([cos1] * MLA_HEADS, axis=1)
    sin8 = jnp.concatenate([sin1] * MLA_HEADS, axis=1)
    nq = _rmsnorm(cq_ref[...], qn_ref[...]).astype(BF16)
    qq = _dot(nq, wq_ref[...])
    width = MLA_HEADS * LANES
    scale = (MLA_NOPE + MLA_ROPE) ** -0.5
    q_ref[...] = ((qq[:, :width] * cos8 + qq[:, width:] * sin8) * scale).astype(BF16)
    nkv = _rmsnorm(ckv_ref[...], kvn_ref[...]).astype(BF16)
    kr = kra_ref[...] * cos1 + krb_ref[...] * sin1
    k_ref[...] = (_dot(nkv, wk_ref[...]) + jnp.concatenate([kr] * MLA_HEADS, axis=1)).astype(BF16)
    vt_ref[...] = _dot_nt(wvt_ref[...], nkv).astype(BF16)


def _mla_proj(proj, cos_t, sin_t, q_norm, kv_norm, wq_cat, wk_pad, wv_t, tm=512):
    T = proj.shape[0]
    width = MLA_HEADS * LANES
    full = lambda shape: pl.BlockSpec(shape, lambda i: (0,) * len(shape))
    return pl.pallas_call(
        _mla_proj_kernel,
        grid=(T // tm,),
        in_specs=[
            pl.BlockSpec((tm, MLA_Q_RANK), lambda i: (i, COL_CQ // MLA_Q_RANK)),
            pl.BlockSpec((tm, LANES), lambda i: (i, COL_CKV // LANES)),
            pl.BlockSpec((tm, LANES), lambda i: (i, COL_KRA // LANES)),
            pl.BlockSpec((tm, LANES), lambda i: (i, COL_KRB // LANES)),
            pl.BlockSpec((tm, LANES), lambda i: (i, 0)),
            pl.BlockSpec((tm, LANES), lambda i: (i, 0)),
            full((1, MLA_Q_RANK)),
            full((1, MLA_KV_RANK)),
            full((MLA_Q_RANK, 2 * width)),
            full((MLA_KV_RANK, width)),
            full((MLA_WIDTH, MLA_KV_RANK)),
        ],
        out_specs=[
            pl.BlockSpec((tm, width), lambda i: (i, 0)),
            pl.BlockSpec((tm, width), lambda i: (i, 0)),
            pl.BlockSpec((MLA_WIDTH, tm), lambda i: (0, i)),
        ],
        out_shape=[
            jax.ShapeDtypeStruct((T, width), BF16),
            jax.ShapeDtypeStruct((T, width), BF16),
            jax.ShapeDtypeStruct((MLA_WIDTH, T), BF16),
        ],
        compiler_params=_cparams(("parallel",)),
        name="mla_proj",
    )(proj, proj, proj, proj, cos_t, sin_t, q_norm, kv_norm, wq_cat, wk_pad, wv_t)


def _attn_kernel(q_ref, k_ref, vt_ref, g_ref, o_ref, *, blk):
    qi = pl.program_id(2)
    n_part = 1
    qw = blk // n_part
    kpos = lax.broadcasted_iota(I32, (blk, qw), 0)
    qpos = lax.broadcasted_iota(I32, (blk, qw), 1)
    gain = jnp.concatenate([g_ref[...]] * (blk // LANES), axis=1)
    qs = [[q_ref[part * qw:(part + 1) * qw, h * LANES:(h + 1) * LANES] for part in range(n_part)] for h in range(2)]

    def step(j, carry, masked):
        off = pl.multiple_of(j * blk, blk)
        new = []
        for h in range(2):
            kb = k_ref[pl.ds(off, blk), h * LANES:(h + 1) * LANES]
            vt = vt_ref[h * MLA_V:(h + 1) * MLA_V, pl.ds(off, blk)]
            for part in range(n_part):
                c0 = 3 * (h * n_part + part)
                m, l, acc = carry[c0:c0 + 3]
                st = _dot_nt(kb, qs[h][part])
                if masked:
                    st = jnp.where(kpos <= qpos + part * qw, st, MASK_VALUE)
                m_new = jnp.maximum(m, jnp.max(st, axis=0, keepdims=True))
                p = jnp.exp(st - m_new)
                alpha = jnp.exp(m - m_new)
                l = alpha * l + jnp.sum(p, axis=0, keepdims=True)
                acc = alpha * acc + _dot(vt, p.astype(BF16))
                new += [m_new, l, acc]
        return tuple(new)

    init = (jnp.full((1, qw), -jnp.inf, F32), jnp.zeros((1, qw), F32), jnp.zeros((MLA_V, qw), F32)) * (2 * n_part)
    carry = lax.fori_loop(0, qi, functools.partial(step, masked=False), init)
    carry = step(qi, carry, True)
    outs = []
    for h in range(2):
        parts = [carry[3 * (h * n_part + part):3 * (h * n_part + part) + 3] for part in range(n_part)]
        l = jnp.concatenate([pt[1] for pt in parts], axis=1)
        acc = jnp.concatenate([pt[2] for pt in parts], axis=1)
        o = acc / l
        ms = jnp.mean(o * o, axis=0, keepdims=True)
        outs.append(o * lax.rsqrt(ms + NORM_EPS) * gain[h * MLA_V:(h + 1) * MLA_V])
    o_ref[...] = jnp.concatenate(outs, axis=0).T.astype(BF16)


def _attention(q, k, vt, gain_b, batch, seq, blk=512):
    T = q.shape[0]
    nq = seq // blk
    return pl.pallas_call(
        functools.partial(_attn_kernel, blk=blk),
        grid=(batch, MLA_HEADS // 2, nq),
        in_specs=[
            pl.BlockSpec((blk, 2 * LANES), lambda b, hp, i: (b * nq + i, hp)),
            pl.BlockSpec((seq, 2 * LANES), lambda b, hp, i: (b, hp)),
            pl.BlockSpec((2 * MLA_V, seq), lambda b, hp, i: (hp, b)),
            pl.BlockSpec((2 * MLA_V, LANES), lambda b, hp, i: (hp, 0)),
        ],
        out_specs=pl.BlockSpec((blk, 2 * MLA_V), lambda b, hp, i: (b * nq + i, hp)),
        out_shape=jax.ShapeDtypeStruct((T, MLA_WIDTH), BF16),
        compiler_params=_cparams(("parallel", "parallel", "arbitrary")),
        name="mla_attention",
    )(q, k, vt, gain_b)


def _hgrn_sum_matrix():
    c = HG_CHUNK
    t = np.arange(c)[:, None]
    j = np.arange(c)[None, :]
    blocks = [(j <= t), (j > t)]
    for lv in range(HG_LEVELS):
        m = c >> (lv + 1)
        mid = (t // (2 * m)) * 2 * m + m - 1
        right = t > mid
        blocks.append(np.where(right, (j > mid) & (j <= t), (j > t) & (j <= mid)))
    mat = np.concatenate(blocks, axis=0).astype(np.float32)
    return np.concatenate([mat, mat, mat], axis=1)


def _hgrn_kernel(hq_ref, hf_ref, hi_ref, hit_ref, hg_ref, lb_ref, gain_ref, msum_ref, o_ref, st_ref, *, cb):
    c = HG_CHUNK

    @pl.when(pl.program_id(1) == 0)
    def _():
        st_ref[...] = jnp.zeros_like(st_ref)

    ones_bd = _group_ones(HG_WIDTH)
    lb = lb_ref[...]
    t_row = lax.broadcasted_iota(I32, (c, HG_WIDTH), 0)
    tt = lax.broadcasted_iota(I32, (c, c), 0)
    ss = lax.broadcasted_iota(I32, (c, c), 1)
    lane = lax.broadcasted_iota(I32, (c, LANES), 1)
    bd_mask = (tt // HEAD_DIM) == (ss // HEAD_DIM)
    level_masks = []
    for lv in range(HG_LEVELS):
        m = c >> (lv + 1)
        same_block = (tt // (2 * m)) == (ss // (2 * m))
        level_masks.append(same_block & ((tt & m) != 0) & ((ss & m) == 0))

    for ch in range(cb // c):
        rows = slice(ch * c, (ch + 1) * c)
        fx = hf_ref[rows, :]
        f_gate = lb + (1.0 - lb) * jax.nn.sigmoid(fx)
        lf = jnp.log(jnp.maximum(f_gate, GATE_FLOOR))
        kf = (1.0 - lb) * jax.nn.sigmoid(-fx)
        qf = jax.nn.silu(hq_ref[rows, :])
        vf = hi_ref[rows, :]
        lf_hi = lf.astype(BF16)
        r1 = lf - lf_hi.astype(F32)
        lf_mid = r1.astype(BF16)
        lf_lo = (r1 - lf_mid.astype(F32)).astype(BF16)
        e = jnp.exp(_dot(msum_ref[...], jnp.concatenate([lf_hi, lf_mid, lf_lo], axis=0)))
        eb = e[0:c]
        es = e[c:2 * c]
        q_in = (qf * eb).astype(BF16)
        k_out = (kf * es).astype(BF16)
        ys = []
        for lv in range(HG_LEVELS):
            m = c >> (lv + 1)
            ys.append(jnp.where((t_row & m) != 0, qf, kf) * e[(2 + lv) * c:(3 + lv) * c])
        o_parts = []
        for pair in range(HG_HEADS // 2):
            lanes = slice(pair * LANES, (pair + 1) * LANES)
            state_t = st_ref[pair]
            o_inter = _dot_nt(q_in[:, lanes], state_t.astype(BF16))
            a = [jnp.zeros((c, c), F32), jnp.zeros((c, c), F32)]
            for lv in range(HG_LEVELS):
                y = ys[lv][:, lanes]
                yb = y.astype(BF16)
                for h in range(2):
                    yh = jnp.where((lane // HEAD_DIM) == h, y, 0.0).astype(BF16)
                    a[h] = a[h] + jnp.where(level_masks[lv], _dot_nt(yh, yb), 0.0)
            vp = vf[:, lanes].astype(BF16)
            o_intra = jnp.where(lane < HEAD_DIM, _dot(a[0].astype(BF16), vp), _dot(a[1].astype(BF16), vp))
            o_parts.append(o_inter + o_intra)
            upd = _dot(hit_ref[lanes, rows].astype(BF16), k_out[:, lanes])
            st_ref[pair] = state_t * eb[c - 1:c, lanes] + jnp.where(bd_mask, upd, 0.0)
        o = jnp.concatenate(o_parts, axis=1) + _group_sum(qf * kf, ones_bd) * vf
        y_out = _head_rmsnorm(o, gain_ref[...], ones_bd) * jax.nn.silu(hg_ref[rows, :])
        o_ref[rows, :] = y_out.astype(BF16)


def _hgrn(proj, hit, lb, gain, msum, batch, seq, cb=512):
    T = proj.shape[0]
    nb = seq // cb
    col = lambda off: pl.BlockSpec((cb, HG_WIDTH), lambda b, i: (b * nb + i, off // HG_WIDTH))
    return pl.pallas_call(
        functools.partial(_hgrn_kernel, cb=cb),
        grid=(batch, nb),
        in_specs=[
            col(COL_HQ), col(COL_HF), col(COL_HI),
            pl.BlockSpec((HG_WIDTH, cb), lambda b, i: (0, b * nb + i)),
            col(COL_HG),
            pl.BlockSpec((1, HG_WIDTH), lambda b, i: (0, 0)),
            pl.BlockSpec((1, HG_WIDTH), lambda b, i: (0, 0)),
            pl.BlockSpec(msum.shape, lambda b, i: (0, 0)),
        ],
        out_specs=pl.BlockSpec((cb, HG_WIDTH), lambda b, i: (b * nb + i, 0)),
        out_shape=jax.ShapeDtypeStruct((T, HG_WIDTH), BF16),
        scratch_shapes=[pltpu.VMEM((HG_HEADS // 2, LANES, LANES), F32)],
        compiler_params=_cparams(("parallel", "arbitrary")),
        name="hgrn2",
    )(proj, proj, proj, hit, proj, lb, gain, msum)


def _outproj_kernel(ya_ref, yb_ref, yc_ref, x_ref, wa_ref, wb_ref, wc_ref, g_ref, wqt_ref, sk_ref,
                    x1_ref, xn_ref, s_ref):
    x1 = x_ref[...] + _dot(ya_ref[...], wa_ref[...]) + _dot(yb_ref[...], wb_ref[...]) + _dot(yc_ref[...], wc_ref[...])
    x1_ref[...] = x1
    xn = _rmsnorm(x1, g_ref[...]).astype(BF16)
    xn_ref[...] = xn
    qt = _dot_nt(wqt_ref[...], xn)
    half = PEER_QDIM // 2
    for hp in range(2 * PEER_HEADS):
        s_ref[hp] = _dot(sk_ref[hp], qt[hp * half:(hp + 1) * half].astype(BF16))


def _outproj(ya, yb, yc, x, wa, wb, wc, g, wq_t, sub_keys, tm=512):
    T = x.shape[0]
    full = lambda shape: pl.BlockSpec(shape, lambda i: (0,) * len(shape))
    row = lambda w: pl.BlockSpec((tm, w), lambda i: (i, 0))
    return pl.pallas_call(
        _outproj_kernel,
        grid=(T // tm,),
        in_specs=[
            row(GM_WIDTH), row(MLA_WIDTH), row(HG_WIDTH), row(D_MODEL),
            full((GM_WIDTH, D_MODEL)), full((MLA_WIDTH, D_MODEL)), full((HG_WIDTH, D_MODEL)),
            full((1, D_MODEL)), full((D_MODEL, D_MODEL)),
            full((2 * PEER_HEADS, PEER_N_KEYS, PEER_QDIM // 2)),
        ],
        out_specs=[
            row(D_MODEL), row(D_MODEL),
            pl.BlockSpec((2 * PEER_HEADS, PEER_N_KEYS, tm), lambda i: (0, 0, i)),
        ],
        out_shape=[
            jax.ShapeDtypeStruct((T, D_MODEL), F32),
            jax.ShapeDtypeStruct((T, D_MODEL), BF16),
            jax.ShapeDtypeStruct((2 * PEER_HEADS, PEER_N_KEYS, T), F32),
        ],
        compiler_params=_cparams(("parallel",)),
        name="outproj_peer_scores",
    )(ya, yb, yc, x, wa, wb, wc, g, wq_t, sub_keys)


_FRONTIER = [(r, c) for r in range(PEER_TOPK) for c in range(PEER_TOPK) if (r + 1) * (c + 1) <= PEER_TOPK]


def _topk_kernel(s_ref, ia_ref, ib_ref, g_ref, work_ref, val_ref, idx_ref):
    neg_inf = jnp.full((8, LANES), -jnp.inf, F32)
    zero = jnp.zeros((8, LANES), F32)

    def scan(values, ids):
        m, e = neg_inf, zero
        for v, i in zip(values, ids):
            gt = v > m
            m = jnp.maximum(m, v)
            e = jnp.where(gt, i, e)
        return m, e

    def merge(parts):
        m, e = parts[0]
        for m2, e2 in parts[1:]:
            gt = m2 > m
            m = jnp.maximum(m, m2)
            e = jnp.where(gt, e2, e)
        return m, e

    n_chain = 8
    per_chain = PEER_N_KEYS // n_chain
    for p in range(2):
        work_ref[...] = s_ref[p]

        def extract(r, prev, p=p):
            parts = []
            for ch in range(n_chain):
                vals = []
                for k in range(ch * per_chain, (ch + 1) * per_chain):
                    s = jnp.where(prev == float(k), -jnp.inf, work_ref[k])
                    work_ref[k] = s
                    vals.append(s)
                parts.append(scan(vals, [float(k) for k in range(ch * per_chain, (ch + 1) * per_chain)]))
            m, idx = merge(parts)
            val_ref[p, r] = m
            idx_ref[p, r] = idx.astype(I32)
            return idx

        lax.fori_loop(0, PEER_TOPK, extract, jnp.full((8, LANES), -1.0, F32))

    va = [val_ref[0, r] for r in range(PEER_TOPK)]
    vb = [val_ref[1, r] for r in range(PEER_TOPK)]
    ia = [idx_ref[0, r] for r in range(PEER_TOPK)]
    ib = [idx_ref[1, r] for r in range(PEER_TOPK)]
    cand = [va[r] + vb[c] for r, c in _FRONTIER]
    cid = [(ia[r] * PEER_N_KEYS + ib[c]).astype(F32) for r, c in _FRONTIER]
    n_cand = len(_FRONTIER)
    bounds = [n_cand * i // 4 for i in range(5)]
    best = []
    for k in range(PEER_TOPK):
        m, eid_f = merge([scan(cand[lo:hi], cid[lo:hi]) for lo, hi in zip(bounds[:-1], bounds[1:])])
        cand = [jnp.where(ci == eid_f, -jnp.inf, cv) for ci, cv in zip(cid, cand)]
        best.append(m)
        eid = eid_f.astype(I32)
        ia_ref[0, k] = eid >> 7
        ib_ref[0, k] = eid & (PEER_N_KEYS - 1)
    ex = [jnp.exp(b - best[0]) for b in best]
    inv = 1.0 / functools.reduce(jnp.add, ex)
    for k in range(PEER_TOPK):
        g_ref[0, k] = ex[k] * inv


def _topk(scores4):
    nt = scores4.shape[2]
    out_spec = pl.BlockSpec((1, PEER_TOPK, 8, LANES), lambda i, h: (h, 0, i, 0))
    out_sds = lambda dt: jax.ShapeDtypeStruct((PEER_HEADS, PEER_TOPK, nt, LANES), dt)
    return pl.pallas_call(
        _topk_kernel,
        grid=(nt // 8, PEER_HEADS),
        in_specs=[pl.BlockSpec((2, PEER_N_KEYS, 8, LANES), lambda i, h: (h, 0, i, 0))],
        out_specs=[out_spec, out_spec, out_spec],
        out_shape=[out_sds(I32), out_sds(I32), out_sds(F32)],
        scratch_shapes=[
            pltpu.VMEM((PEER_N_KEYS, 8, LANES), F32),
            pltpu.VMEM((2, PEER_TOPK, 8, LANES), F32),
            pltpu.VMEM((2, PEER_TOPK, 8, LANES), I32),
        ],
        compiler_params=_cparams(("parallel", "parallel")),
        name="peer_topk",
    )(scores4)


def _peer_kernel(xn_ref, x1_ref, a_ref, b_ref, g_ref, ut_ref, v_ref, o_ref, wt_ref, h_ref, at_ref, bt_ref, gt_ref,
                 *, tm, te, grp):
    j = pl.program_id(1)
    n_sub = te // PEER_N_KEYS
    half_keys = PEER_N_KEYS // 2

    @pl.when(j == 0)
    def _():
        o_ref[...] = x1_ref[...]
        at_ref[...] = a_ref[...].astype(F32).T
        bt_ref[...] = b_ref[...].astype(F32).T
        gt_ref[...] = g_ref[...].T
        key_iota = lax.broadcasted_iota(I32, (PEER_N_KEYS, LANES), 0).astype(F32).astype(BF16)
        one = jnp.ones((PEER_N_KEYS, LANES), BF16)
        zero = jnp.zeros((PEER_N_KEYS, LANES), BF16)

        def build(i, _):
            t0 = pl.multiple_of(i * grp, grp)
            for r in range(grp):
                a_row = at_ref[pl.ds(t0 + r, 1), :].astype(BF16)
                b_row = bt_ref[pl.ds(t0 + r, 1), :].astype(BF16)
                g_row = gt_ref[pl.ds(t0 + r, 1), :].astype(BF16)
                oa = jnp.where(key_iota == a_row, one, zero)
                gob = jnp.where(key_iota == b_row, jnp.broadcast_to(g_row, one.shape), zero)
                bits = pltpu.bitcast(_dot_nt(oa, gob), U32)
                word = (bits[half_keys:] & HI16) | lax.shift_right_logical(bits[:half_keys], jnp.uint32(16))
                wt_ref[pl.ds(pl.multiple_of((t0 + r) * W_PITCH, 8), half_keys), :] = word
            return 0

        lax.fori_loop(0, tm // grp, build, 0)

    act = jax.nn.gelu(_dot(xn_ref[...], ut_ref[...]))
    first = j * n_sub
    row0 = first % half_keys
    shift = (jnp.uint32(16) * (1 - first // half_keys).astype(U32))
    for i in range(n_sub):
        words = wt_ref[pl.ds(row0 + i, tm, stride=W_PITCH), :]
        gate = pltpu.bitcast(lax.shift_left(words, shift) & HI16, F32)
        h_ref[:, i * LANES:(i + 1) * LANES] = (act[:, i * LANES:(i + 1) * LANES] * gate).astype(BF16)
    o_ref[...] += _dot(h_ref[...], v_ref[...])


def _peer(xn, x1, ids_a, ids_b, gates, u_t, v_tab, tm=512, te=2048, grp=16):
    T = xn.shape[0]
    n_sel = PEER_HEADS * PEER_TOPK
    row = lambda w: pl.BlockSpec((tm, w), lambda i, j: (i, 0))
    sel = pl.BlockSpec((n_sel, tm), lambda i, j: (0, i))
    return pl.pallas_call(
        functools.partial(_peer_kernel, tm=tm, te=te, grp=grp),
        grid=(T // tm, PEER_N_EXPERTS // te),
        in_specs=[
            row(D_MODEL), row(D_MODEL), sel, sel, sel,
            pl.BlockSpec((D_MODEL, te), lambda i, j: (0, j)),
            pl.BlockSpec((te, D_MODEL), lambda i, j: (j, 0)),
        ],
        out_specs=row(D_MODEL),
        out_shape=jax.ShapeDtypeStruct((T, D_MODEL), F32),
        scratch_shapes=[
            pltpu.VMEM((tm * W_PITCH, LANES), U32),
            pltpu.VMEM((tm, te), BF16),
            pltpu.VMEM((tm, n_sel), F32),
            pltpu.VMEM((tm, n_sel), F32),
            pltpu.VMEM((tm, n_sel), F32),
        ],
        compiler_params=_cparams(("parallel", "arbitrary")),
        name="peer_mix",
    )(xn, x1, ids_a, ids_b, gates, u_t, v_tab)


def _final_norm_kernel(x_ref, g_ref, o_ref):
    o_ref[...] = _rmsnorm(x_ref[...], g_ref[...])


def _final_norm(x, g, tm=512):
    T = x.shape[0]
    return pl.pallas_call(
        _final_norm_kernel,
        grid=(T // tm,),
        in_specs=[pl.BlockSpec((tm, D_MODEL), lambda i: (i, 0)), pl.BlockSpec((1, D_MODEL), lambda i: (0, 0))],
        out_specs=pl.BlockSpec((tm, D_MODEL), lambda i: (i, 0)),
        out_shape=jax.ShapeDtypeStruct((T, D_MODEL), F32),
        compiler_params=_cparams(("parallel",)),
        name="final_norm",
    )(x, g)


def _prep_w_in(w):
    u, v, cq = w[:, 0:256], w[:, 256:512], w[:, 512:768]
    ckv, kr = w[:, 768:896], w[:, 896:928]
    hq, hf, hi, hg = w[:, 928:1184], w[:, 1184:1440], w[:, 1440:1696], w[:, 1696:1952]
    half = MLA_ROPE // 2
    kr_sw = jnp.concatenate([kr[:, half:], kr[:, :half]], axis=1)
    z64 = jnp.zeros((D_MODEL, MLA_NOPE), w.dtype)
    z32 = jnp.zeros((D_MODEL, LANES - MLA_NOPE - MLA_ROPE), w.dtype)
    kra = jnp.concatenate([z64, kr, z32], axis=1)
    krb = jnp.concatenate([z64, kr_sw, z32], axis=1)
    w_perm = jnp.concatenate([u, v, cq, hq, hf, hi, hg, ckv, kra, krb], axis=1).astype(BF16)
    return w_perm, hi.T.astype(BF16)


def _prep_w_uq(w):
    half = MLA_ROPE // 2
    w3 = w.reshape(MLA_Q_RANK, MLA_HEADS, MLA_NOPE + MLA_ROPE)
    nope, rope = w3[:, :, :MLA_NOPE], w3[:, :, MLA_NOPE:]
    rope_sw = jnp.concatenate([rope[:, :, half:], rope[:, :, :half]], axis=-1)
    z32 = jnp.zeros((MLA_Q_RANK, MLA_HEADS, LANES - MLA_NOPE - MLA_ROPE), w.dtype)
    z64 = jnp.zeros((MLA_Q_RANK, MLA_HEADS, MLA_NOPE), w.dtype)
    main = jnp.concatenate([nope, rope, z32], axis=-1).reshape(MLA_Q_RANK, MLA_HEADS * LANES)
    swap = jnp.concatenate([z64, rope_sw, z32], axis=-1).reshape(MLA_Q_RANK, MLA_HEADS * LANES)
    return jnp.concatenate([main, swap], axis=1).astype(BF16)


def _prep_w_ukv(w):
    w3 = w.reshape(MLA_KV_RANK, MLA_HEADS, MLA_NOPE + MLA_V)
    k_nope, v = w3[:, :, :MLA_NOPE], w3[:, :, MLA_NOPE:]
    z = jnp.zeros((MLA_KV_RANK, MLA_HEADS, LANES - MLA_NOPE), w.dtype)
    wk_pad = jnp.concatenate([k_nope, z], axis=-1).reshape(MLA_KV_RANK, MLA_HEADS * LANES).astype(BF16)
    wv_t = v.reshape(MLA_KV_RANK, MLA_WIDTH).T.astype(BF16)
    return wk_pad, wv_t


def kernel(x, positions, norm_mix, w_in, gm_v_norm, gm_ws, gm_b, gm_out_norm, mla_q_norm, mla_w_uq, mla_kv_norm, mla_w_ukv, mla_out_norm, hg_lb_logits, hg_out_norm, w_out, norm_ffn, peer_w_q, peer_sub_keys, peer_u, peer_v, norm_final):
    batch, seq, _ = x.shape
    T = batch * seq
    depth = w_in.shape[0]
    xt = x.reshape(T, D_MODEL)
    pos_b = jnp.broadcast_to(positions.reshape(T, 1).astype(F32), (T, LANES))
    cos_t, sin_t = _rope_tables(pos_b)
    p = jax.nn.softmax(hg_lb_logits.astype(F32), axis=0)
    lower = jnp.cumsum(p, axis=0) - p[0]
    msum = jnp.asarray(_hgrn_sum_matrix(), BF16)
    row = lambda v: v.reshape(1, -1)

    for l in range(depth):
        w_perm, w_hit = _prep_w_in(w_in[l])
        proj, hit = _inproj(xt, row(norm_mix[l]), w_perm, w_hit)
        bias2d = jnp.repeat(gm_b[l].T, HEAD_DIM, axis=1)
        ya = _gmlp(proj, row(gm_v_norm[l]), gm_ws[l], bias2d, row(gm_out_norm[l]))
        wk_pad, wv_t = _prep_w_ukv(mla_w_ukv[l])
        q, k, vt = _mla_proj(proj, cos_t, sin_t, row(mla_q_norm[l]), row(mla_kv_norm[l]),
                             _prep_w_uq(mla_w_uq[l]), wk_pad, wv_t)
        gain_b = jnp.broadcast_to(mla_out_norm[l][:, None], (MLA_WIDTH, LANES))
        yb = _attention(q, k, vt, gain_b, batch, seq)
        yc = _hgrn(proj, hit, row(lower[l]), row(hg_out_norm[l]), msum, batch, seq)
        wo = w_out[l].astype(BF16)
        sub_keys = peer_sub_keys[l].reshape(2 * PEER_HEADS, PEER_N_KEYS, PEER_QDIM // 2).astype(BF16)
        x1, xn, scores = _outproj(ya, yb, yc, xt, wo[:GM_WIDTH], wo[GM_WIDTH:GM_WIDTH + MLA_WIDTH],
                                  wo[GM_WIDTH + MLA_WIDTH:], row(norm_ffn[l]), peer_w_q[l].T.astype(BF16), sub_keys)
        ia, ib, gt = _topk(scores.reshape(2 * PEER_HEADS, PEER_N_KEYS, T // LANES, LANES))
        flat = lambda a: a.reshape(PEER_HEADS * PEER_TOPK, T)
        xt = _peer(xn, x1, flat(ia), flat(ib), flat(gt),
                   peer_u[l].T.astype(BF16), peer_v[l].astype(BF16))
    return _final_norm(xt, row(norm_final)).reshape(batch, seq, D_MODEL)
```

```python
import functools
import math

import numpy as np
import jax
import jax.numpy as jnp
from jax import lax
from jax.experimental import pallas as pl
from jax.experimental.pallas import tpu as pltpu

F32 = jnp.float32
BF16 = jnp.bfloat16
I32 = jnp.int32
U32 = jnp.uint32
HI16 = np.uint32(0xFFFF0000)

D_MODEL = 1024
DEPTH = 4
GM_HEADS = 4
GM_WIDTH = 256
GM_CHUNK = 128
MLA_HEADS = 8
MLA_NOPE = 64
MLA_ROPE = 32
MLA_V = 64
MLA_WIDTH = 512
MLA_Q_RANK = 256
MLA_KV_RANK = 128
ROPE_THETA = 10000.0
MASK_VALUE = -1e30
HG_HEADS = 4
HG_WIDTH = 256
GATE_FLOOR = 1e-20
PEER_HEADS = 8
PEER_N_KEYS = 128
PEER_N_EXPERTS = PEER_N_KEYS * PEER_N_KEYS
PEER_TOPK = 16
PEER_QDIM = 128
NORM_EPS = 1e-6

LANES = 128
HEAD_DIM = 64
VMEM_LIMIT = 56 * 1024 * 1024

PROJ_WIDTH = 2176
COL_U, COL_V, COL_CQ, COL_HQ, COL_HF, COL_HI, COL_HG = 0, 256, 512, 768, 1024, 1280, 1536
COL_CKV, COL_KRA, COL_KRB = 1792, 1920, 2048

W_PITCH = 72
HG_CHUNK = 128
HG_LEVELS = 7

_NT = (((1,), (1,)), ((), ()))


def _cparams(sem):
    return pltpu.CompilerParams(dimension_semantics=sem, vmem_limit_bytes=VMEM_LIMIT)


def _dot(a, b):
    return jnp.dot(a, b, preferred_element_type=F32)


def _dot_nt(a, b):
    return lax.dot_general(a, b, _NT, preferred_element_type=F32)


def _split2(x):
    hi = x.astype(BF16)
    lo = (x - hi.astype(F32)).astype(BF16)
    return hi, lo


def _group_ones(width):
    r = lax.broadcasted_iota(I32, (width, width), 0) // HEAD_DIM
    c = lax.broadcasted_iota(I32, (width, width), 1) // HEAD_DIM
    return jnp.where(r == c, 1.0, 0.0).astype(BF16)


def _group_sum(x, ones_bd):
    hi, lo = _split2(x)
    return _dot(hi, ones_bd) + _dot(lo, ones_bd)


def _head_rmsnorm(x, gain, ones_bd):
    ms = _group_sum(x * x, ones_bd) * (1.0 / HEAD_DIM)
    return x * lax.rsqrt(ms + NORM_EPS) * gain


def _rmsnorm(x, gain):
    ms = jnp.mean(x * x, axis=-1, keepdims=True)
    return x * lax.rsqrt(ms + NORM_EPS) * gain


def _inproj_kernel(x_ref, g_ref, w_ref, wt_ref, o_ref, ot_ref):
    n = _rmsnorm(x_ref[...], g_ref[...]).astype(BF16)
    o_ref[...] = _dot(n, w_ref[...])
    ot_ref[...] = _dot_nt(wt_ref[...], n)


def _inproj(x, g, w, w_hit, tm=512):
    T = x.shape[0]
    return pl.pallas_call(
        _inproj_kernel,
        grid=(T // tm,),
        in_specs=[
            pl.BlockSpec((tm, D_MODEL), lambda i: (i, 0)),
            pl.BlockSpec((1, D_MODEL), lambda i: (0, 0)),
            pl.BlockSpec((D_MODEL, PROJ_WIDTH), lambda i: (0, 0)),
            pl.BlockSpec((HG_WIDTH, D_MODEL), lambda i: (0, 0)),
        ],
        out_specs=[
            pl.BlockSpec((tm, PROJ_WIDTH), lambda i: (i, 0)),
            pl.BlockSpec((HG_WIDTH, tm), lambda i: (0, i)),
        ],
        out_shape=[
            jax.ShapeDtypeStruct((T, PROJ_WIDTH), F32),
            jax.ShapeDtypeStruct((HG_WIDTH, T), F32),
        ],
        compiler_params=_cparams(("parallel",)),
        name="inproj",
    )(x, g, w, w_hit)


def _rope_table_kernel(pos_ref, c_ref, s_ref):
    lane = lax.broadcasted_iota(I32, (1, LANES), 1)
    half = MLA_ROPE // 2
    jf = ((lane - MLA_NOPE) & (half - 1)).astype(F32)
    inv_freq = jnp.exp(-math.log(ROPE_THETA) * jf / half)
    ang = pos_ref[...] * inv_freq
    c, s = jnp.cos(ang), jnp.sin(ang)
    in_rope = (lane >= MLA_NOPE) & (lane < MLA_NOPE + MLA_ROPE)
    c_ref[...] = jnp.where(lane < MLA_NOPE, 1.0, jnp.where(in_rope, c, 0.0))
    s_ref[...] = jnp.where(in_rope, jnp.where(lane < MLA_NOPE + half, -s, s), 0.0)


def _rope_tables(pos_b, tm=512):
    T = pos_b.shape[0]
    spec = pl.BlockSpec((tm, LANES), lambda i: (i, 0))
    return pl.pallas_call(
        _rope_table_kernel,
        grid=(T // tm,),
        in_specs=[spec],
        out_specs=[spec, spec],
        out_shape=[jax.ShapeDtypeStruct((T, LANES), F32)] * 2,
        compiler_params=_cparams(("parallel",)),
        name="rope_tables",
    )(pos_b)


def _gmlp_kernel(u_ref, v_ref, vg_ref, ws_ref, bias_ref, og_ref, o_ref, *, tb):
    ones_bd = _group_ones(GM_WIDTH)
    r = lax.broadcasted_iota(I32, (GM_CHUNK, GM_CHUNK), 0)
    c = lax.broadcasted_iota(I32, (GM_CHUNK, GM_CHUNK), 1)
    causal = c <= r
    lane = lax.broadcasted_iota(I32, (GM_CHUNK, LANES), 1)
    ws = [jnp.where(causal, ws_ref[h], 0.0).astype(BF16) for h in range(GM_HEADS)]
    for ch in range(tb // GM_CHUNK):
        rows = slice(ch * GM_CHUNK, (ch + 1) * GM_CHUNK)
        u = jax.nn.gelu(u_ref[rows, :])
        v = _head_rmsnorm(jax.nn.gelu(v_ref[rows, :]), vg_ref[...], ones_bd)
        zs = []
        for pair in range(GM_HEADS // 2):
            vp = v[:, pair * LANES:(pair + 1) * LANES].astype(BF16)
            zs.append(jnp.where(lane < HEAD_DIM, _dot(ws[2 * pair], vp), _dot(ws[2 * pair + 1], vp)))
        z = jnp.concatenate(zs, axis=1) + bias_ref[...]
        o_ref[rows, :] = _head_rmsnorm(u * z, og_ref[...], ones_bd).astype(BF16)


def _gmlp(proj, v_gain, ws, bias2d, out_gain, tb=512):
    T = proj.shape[0]
    return pl.pallas_call(
        functools.partial(_gmlp_kernel, tb=tb),
        grid=(T // tb,),
        in_specs=[
            pl.BlockSpec((tb, GM_WIDTH), lambda i: (i, COL_U // GM_WIDTH)),
            pl.BlockSpec((tb, GM_WIDTH), lambda i: (i, COL_V // GM_WIDTH)),
            pl.BlockSpec((1, GM_WIDTH), lambda i: (0, 0)),
            pl.BlockSpec((GM_HEADS, GM_CHUNK, GM_CHUNK), lambda i: (0, 0, 0)),
            pl.BlockSpec((GM_CHUNK, GM_WIDTH), lambda i: (0, 0)),
            pl.BlockSpec((1, GM_WIDTH), lambda i: (0, 0)),
        ],
        out_specs=pl.BlockSpec((tb, GM_WIDTH), lambda i: (i, 0)),
        out_shape=jax.ShapeDtypeStruct((T, GM_WIDTH), BF16),
        compiler_params=_cparams(("parallel",)),
        name="gmlp",
    )(proj, proj, v_gain, ws, bias2d, out_gain)


def _mla_proj_kernel(cq_ref, ckv_ref, kra_ref, krb_ref, c_ref, s_ref, qn_ref, kvn_ref,
                     wq_ref, wk_ref, wvt_ref, q_ref, k_ref, vt_ref):
    cos1, sin1 = c_ref[...], s_ref[...]
    cos8 = jnp.concatenate([cos1] * MLA_HEADS, axis=1)
    sin8 = jnp.concatenate([sin1] * MLA_HEADS, axis=1)
    nq = _rmsnorm(cq_ref[...], qn_ref[...]).astype(BF16)
    qq = _dot(nq, wq_ref[...])
    width = MLA_HEADS * LANES
    scale = (MLA_NOPE + MLA_ROPE) ** -0.5
    q_ref[...] = ((qq[:, :width] * cos8 + qq[:, width:] * sin8) * scale).astype(BF16)
    nkv = _rmsnorm(ckv_ref[...], kvn_ref[...]).astype(BF16)
    kr = kra_ref[...] * cos1 + krb_ref[...] * sin1
    k_ref[...] = (_dot(nkv, wk_ref[...]) + jnp.concatenate([kr] * MLA_HEADS, axis=1)).astype(BF16)
    vt_ref[...] = _dot_nt(wvt_ref[...], nkv).astype(BF16)


def _mla_proj(proj, cos_t, sin_t, q_norm, kv_norm, wq_cat, wk_pad, wv_t, tm=512):
    T = proj.shape[0]
    width = MLA_HEADS * LANES
    full = lambda shape: pl.BlockSpec(shape, lambda i: (0,) * len(shape))
    return pl.pallas_call(
        _mla_proj_kernel,
        grid=(T // tm,),
        in_specs=[
            pl.BlockSpec((tm, MLA_Q_RANK), lambda i: (i, COL_CQ // MLA_Q_RANK)),
            pl.BlockSpec((tm, LANES), lambda i: (i, COL_CKV // LANES)),
            pl.BlockSpec((tm, LANES), lambda i: (i, COL_KRA // LANES)),
            pl.BlockSpec((tm, LANES), lambda i: (i, COL_KRB // LANES)),
            pl.BlockSpec((tm, LANES), lambda i: (i, 0)),
            pl.BlockSpec((tm, LANES), lambda i: (i, 0)),
            full((1, MLA_Q_RANK)),
            full((1, MLA_KV_RANK)),
            full((MLA_Q_RANK, 2 * width)),
            full((MLA_KV_RANK, width)),
            full((MLA_WIDTH, MLA_KV_RANK)),
        ],
        out_specs=[
            pl.BlockSpec((tm, width), lambda i: (i, 0)),
            pl.BlockSpec((tm, width), lambda i: (i, 0)),
            pl.BlockSpec((MLA_WIDTH, tm), lambda i: (0, i)),
        ],
        out_shape=[
            jax.ShapeDtypeStruct((T, width), BF16),
            jax.ShapeDtypeStruct((T, width), BF16),
            jax.ShapeDtypeStruct((MLA_WIDTH, T), BF16),
        ],
        compiler_params=_cparams(("parallel",)),
        name="mla_proj",
    )(proj, proj, proj, proj, cos_t, sin_t, q_norm, kv_norm, wq_cat, wk_pad, wv_t)


def _attn_kernel(q_ref, k_ref, vt_ref, g_ref, o_ref, *, blk):
    qi = pl.program_id(2)
    n_part = 1
    qw = blk // n_part
    kpos = lax.broadcasted_iota(I32, (blk, qw), 0)
    qpos = lax.broadcasted_iota(I32, (blk, qw), 1)
    gain = jnp.concatenate([g_ref[...]] * (blk // LANES), axis=1)
    qs = [[q_ref[part * qw:(part + 1) * qw, h * LANES:(h + 1) * LANES] for part in range(n_part)] for h in range(2)]

    def step(j, carry, masked):
        off = pl.multiple_of(j * blk, blk)
        new = []
        for h in range(2):
            kb = k_ref[pl.ds(off, blk), h * LANES:(h + 1) * LANES]
            vt = vt_ref[h * MLA_V:(h + 1) * MLA_V, pl.ds(off, blk)]
            for part in range(n_part):
                c0 = 3 * (h * n_part + part)
                m, l, acc = carry[c0:c0 + 3]
                st = _dot_nt(kb, qs[h][part])
                if masked:
                    st = jnp.where(kpos <= qpos + part * qw, st, MASK_VALUE)
                m_new = jnp.maximum(m, jnp.max(st, axis=0, keepdims=True))
                p = jnp.exp(st - m_new)
                alpha = jnp.exp(m - m_new)
                l = alpha * l + jnp.sum(p, axis=0, keepdims=True)
                acc = alpha * acc + _dot(vt, p.astype(BF16))
                new += [m_new, l, acc]
        return tuple(new)

    init = (jnp.full((1, qw), -jnp.inf, F32), jnp.zeros((1, qw), F32), jnp.zeros((MLA_V, qw), F32)) * (2 * n_part)
    carry = lax.fori_loop(0, qi, functools.partial(step, masked=False), init)
    carry = step(qi, carry, True)
    outs = []
    for h in range(2):
        parts = [carry[3 * (h * n_part + part):3 * (h * n_part + part) + 3] for part in range(n_part)]
        l = jnp.concatenate([pt[1] for pt in parts], axis=1)
        acc = jnp.concatenate([pt[2] for pt in parts], axis=1)
        o = acc / l
        ms = jnp.mean(o * o, axis=0, keepdims=True)
        outs.append(o * lax.rsqrt(ms + NORM_EPS) * gain[h * MLA_V:(h + 1) * MLA_V])
    o_ref[...] = jnp.concatenate(outs, axis=0).T.astype(BF16)


def _attention(q, k, vt, gain_b, batch, seq, blk=512):
    T = q.shape[0]
    nq = seq // blk
    return pl.pallas_call(
        functools.partial(_attn_kernel, blk=blk),
        grid=(batch, MLA_HEADS // 2, nq),
        in_specs=[
            pl.BlockSpec((blk, 2 * LANES), lambda b, hp, i: (b * nq + i, hp)),
            pl.BlockSpec((seq, 2 * LANES), lambda b, hp, i: (b, hp)),
            pl.BlockSpec((2 * MLA_V, seq), lambda b, hp, i: (hp, b)),
            pl.BlockSpec((2 * MLA_V, LANES), lambda b, hp, i: (hp, 0)),
        ],
        out_specs=pl.BlockSpec((blk, 2 * MLA_V), lambda b, hp, i: (b * nq + i, hp)),
        out_shape=jax.ShapeDtypeStruct((T, MLA_WIDTH), BF16),
        compiler_params=_cparams(("parallel", "parallel", "arbitrary")),
        name="mla_attention",
    )(q, k, vt, gain_b)


def _hgrn_sum_matrix():
    c = HG_CHUNK
    t = np.arange(c)[:, None]
    j = np.arange(c)[None, :]
    blocks = [(j <= t), (j > t)]
    for lv in range(HG_LEVELS):
        m = c >> (lv + 1)
        mid = (t // (2 * m)) * 2 * m + m - 1
        right = t > mid
        blocks.append(np.where(right, (j > mid) & (j <= t), (j > t) & (j <= mid)))
    mat = np.concatenate(blocks, axis=0).astype(np.float32)
    return np.concatenate([mat, mat, mat], axis=1)


def _hgrn_kernel(hq_ref, hf_ref, hi_ref, hit_ref, hg_ref, lb_ref, gain_ref, msum_ref, o_ref, st_ref, *, cb):
    c = HG_CHUNK

    @pl.when(pl.program_id(1) == 0)
    def _():
        st_ref[...] = jnp.zeros_like(st_ref)

    ones_bd = _group_ones(HG_WIDTH)
    lb = lb_ref[...]
    t_row = lax.broadcasted_iota(I32, (c, HG_WIDTH), 0)
    tt = lax.broadcasted_iota(I32, (c, c), 0)
    ss = lax.broadcasted_iota(I32, (c, c), 1)
    lane = lax.broadcasted_iota(I32, (c, LANES), 1)
    bd_mask = (tt // HEAD_DIM) == (ss // HEAD_DIM)
    level_masks = []
    for lv in range(HG_LEVELS):
        m = c >> (lv + 1)
        same_block = (tt // (2 * m)) == (ss // (2 * m))
        level_masks.append(same_block & ((tt & m) != 0) & ((ss & m) == 0))

    for ch in range(cb // c):
        rows = slice(ch * c, (ch + 1) * c)
        fx = hf_ref[rows, :]
        f_gate = lb + (1.0 - lb) * jax.nn.sigmoid(fx)
        lf = jnp.log(jnp.maximum(f_gate, GATE_FLOOR))
        kf = (1.0 - lb) * jax.nn.sigmoid(-fx)
        qf = jax.nn.silu(hq_ref[rows, :])
        vf = hi_ref[rows, :]
        lf_hi = lf.astype(BF16)
        r1 = lf - lf_hi.astype(F32)
        lf_mid = r1.astype(BF16)
        lf_lo = (r1 - lf_mid.astype(F32)).astype(BF16)
        e = jnp.exp(_dot(msum_ref[...], jnp.concatenate([lf_hi, lf_mid, lf_lo], axis=0)))
        eb = e[0:c]
        es = e[c:2 * c]
        q_in = (qf * eb).astype(BF16)
        k_out = (kf * es).astype(BF16)
        ys = []
        for lv in range(HG_LEVELS):
            m = c >> (lv + 1)
            ys.append(jnp.where((t_row & m) != 0, qf, kf) * e[(2 + lv) * c:(3 + lv) * c])
        o_parts = []
        for pair in range(HG_HEADS // 2):
            lanes = slice(pair * LANES, (pair + 1) * LANES)
            state_t = st_ref[pair]
            o_inter = _dot_nt(q_in[:, lanes], state_t.astype(BF16))
            a = [jnp.zeros((c, c), F32), jnp.zeros((c, c), F32)]
            for lv in range(HG_LEVELS):
                y = ys[lv][:, lanes]
                yh = jnp.concatenate([jnp.where(lane < HEAD_DIM, y, 0.0), jnp.where(lane < HEAD_DIM, 0.0, y)], axis=0)
                g = _dot_nt(yh.astype(BF16), y.astype(BF16))
                for h in range(2):
                    a[h] = a[h] + jnp.where(level_masks[lv], g[h * c:(h + 1) * c], 0.0)
            vp = vf[:, lanes].astype(BF16)
            av = _dot(jnp.concatenate(a, axis=0).astype(BF16), vp)
            o_intra = jnp.where(lane < HEAD_DIM, av[:c], av[c:])
            o_parts.append(o_inter + o_intra)
            upd = _dot(hit_ref[lanes, rows].astype(BF16), k_out[:, lanes])
            st_ref[pair] = state_t * eb[c - 1:c, lanes] + jnp.where(bd_mask, upd, 0.0)
        o = jnp.concatenate(o_parts, axis=1) + _group_sum(qf * kf, ones_bd) * vf
        y_out = _head_rmsnorm(o, gain_ref[...], ones_bd) * jax.nn.silu(hg_ref[rows, :])
        o_ref[rows, :] = y_out.astype(BF16)


def _hgrn(proj, hit, lb, gain, msum, batch, seq, cb=512):
    T = proj.shape[0]
    nb = seq // cb
    col = lambda off: pl.BlockSpec((cb, HG_WIDTH), lambda b, i: (b * nb + i, off // HG_WIDTH))
    return pl.pallas_call(
        functools.partial(_hgrn_kernel, cb=cb),
        grid=(batch, nb),
        in_specs=[
            col(COL_HQ), col(COL_HF), col(COL_HI),
            pl.BlockSpec((HG_WIDTH, cb), lambda b, i: (0, b * nb + i)),
            col(COL_HG),
            pl.BlockSpec((1, HG_WIDTH), lambda b, i: (0, 0)),
            pl.BlockSpec((1, HG_WIDTH), lambda b, i: (0, 0)),
            pl.BlockSpec(msum.shape, lambda b, i: (0, 0)),
        ],
        out_specs=pl.BlockSpec((cb, HG_WIDTH), lambda b, i: (b * nb + i, 0)),
        out_shape=jax.ShapeDtypeStruct((T, HG_WIDTH), BF16),
        scratch_shapes=[pltpu.VMEM((HG_HEADS // 2, LANES, LANES), F32)],
        compiler_params=_cparams(("parallel", "arbitrary")),
        name="hgrn2",
    )(proj, proj, proj, hit, proj, lb, gain, msum)


def _outproj_kernel(ya_ref, yb_ref, yc_ref, x_ref, wa_ref, wb_ref, wc_ref, g_ref, wqt_ref, sk_ref,
                    x1_ref, xn_ref, s_ref):
    x1 = x_ref[...] + _dot(ya_ref[...], wa_ref[...]) + _dot(yb_ref[...], wb_ref[...]) + _dot(yc_ref[...], wc_ref[...])
    x1_ref[...] = x1
    xn = _rmsnorm(x1, g_ref[...]).astype(BF16)
    xn_ref[...] = xn
    qt = _dot_nt(wqt_ref[...], xn)
    half = PEER_QDIM // 2
    for hp in range(2 * PEER_HEADS):
        s_ref[hp] = _dot(sk_ref[hp], qt[hp * half:(hp + 1) * half].astype(BF16))


def _outproj(ya, yb, yc, x, wa, wb, wc, g, wq_t, sub_keys, tm=512):
    T = x.shape[0]
    full = lambda shape: pl.BlockSpec(shape, lambda i: (0,) * len(shape))
    row = lambda w: pl.BlockSpec((tm, w), lambda i: (i, 0))
    return pl.pallas_call(
        _outproj_kernel,
        grid=(T // tm,),
        in_specs=[
            row(GM_WIDTH), row(MLA_WIDTH), row(HG_WIDTH), row(D_MODEL),
            full((GM_WIDTH, D_MODEL)), full((MLA_WIDTH, D_MODEL)), full((HG_WIDTH, D_MODEL)),
            full((1, D_MODEL)), full((D_MODEL, D_MODEL)),
            full((2 * PEER_HEADS, PEER_N_KEYS, PEER_QDIM // 2)),
        ],
        out_specs=[
            row(D_MODEL), row(D_MODEL),
            pl.BlockSpec((2 * PEER_HEADS, PEER_N_KEYS, tm), lambda i: (0, 0, i)),
        ],
        out_shape=[
            jax.ShapeDtypeStruct((T, D_MODEL), F32),
            jax.ShapeDtypeStruct((T, D_MODEL), BF16),
            jax.ShapeDtypeStruct((2 * PEER_HEADS, PEER_N_KEYS, T), F32),
        ],
        compiler_params=_cparams(("parallel",)),
        name="outproj_peer_scores",
    )(ya, yb, yc, x, wa, wb, wc, g, wq_t, sub_keys)


_FRONTIER = [(r, c) for r in range(PEER_TOPK) for c in range(PEER_TOPK) if (r + 1) * (c + 1) <= PEER_TOPK]


def _topk_kernel(s_ref, ia_ref, ib_ref, g_ref, work_ref, val_ref, idx_ref):
    neg_inf = jnp.full((8, LANES), -jnp.inf, F32)
    zero = jnp.zeros((8, LANES), F32)

    def scan(values, ids):
        m, e = neg_inf, zero
        for v, i in zip(values, ids):
            gt = v > m
            m = jnp.maximum(m, v)
            e = jnp.where(gt, i, e)
        return m, e

    def merge(parts):
        m, e = parts[0]
        for m2, e2 in parts[1:]:
            gt = m2 > m
            m = jnp.maximum(m, m2)
            e = jnp.where(gt, e2, e)
        return m, e

    n_chain = 8
    per_chain = PEER_N_KEYS // n_chain
    for p in range(2):
        work_ref[...] = s_ref[p]

        def extract(r, prev, p=p):
            parts = []
            for ch in range(n_chain):
                vals = []
                for k in range(ch * per_chain, (ch + 1) * per_chain):
                    s = jnp.where(prev == float(k), -jnp.inf, work_ref[k])
                    work_ref[k] = s
                    vals.append(s)
                parts.append(scan(vals, [float(k) for k in range(ch * per_chain, (ch + 1) * per_chain)]))
            m, idx = merge(parts)
            val_ref[p, r] = m
            idx_ref[p, r] = idx.astype(I32)
            return idx

        lax.fori_loop(0, PEER_TOPK, extract, jnp.full((8, LANES), -1.0, F32))

    va = [val_ref[0, r] for r in range(PEER_TOPK)]
    vb = [val_ref[1, r] for r in range(PEER_TOPK)]
    ia = [idx_ref[0, r] for r in range(PEER_TOPK)]
    ib = [idx_ref[1, r] for r in range(PEER_TOPK)]
    cand = [va[r] + vb[c] for r, c in _FRONTIER]
    cid = [(ia[r] * PEER_N_KEYS + ib[c]).astype(F32) for r, c in _FRONTIER]
    n_cand = len(_FRONTIER)
    bounds = [n_cand * i // 4 for i in range(5)]
    best = []
    for k in range(PEER_TOPK):
        m, eid_f = merge([scan(cand[lo:hi], cid[lo:hi]) for lo, hi in zip(bounds[:-1], bounds[1:])])
        cand = [jnp.where(ci == eid_f, -jnp.inf, cv) for ci, cv in zip(cid, cand)]
        best.append(m)
        eid = eid_f.astype(I32)
        ia_ref[0, k] = eid >> 7
        ib_ref[0, k] = eid & (PEER_N_KEYS - 1)
    ex = [jnp.exp(b - best[0]) for b in best]
    inv = 1.0 / functools.reduce(jnp.add, ex)
    for k in range(PEER_TOPK):
        g_ref[0, k] = ex[k] * inv


def _topk(scores4):
    nt = scores4.shape[2]
    out_spec = pl.BlockSpec((1, PEER_TOPK, 8, LANES), lambda i, h: (h, 0, i, 0))
    out_sds = lambda dt: jax.ShapeDtypeStruct((PEER_HEADS, PEER_TOPK, nt, LANES), dt)
    return pl.pallas_call(
        _topk_kernel,
        grid=(nt // 8, PEER_HEADS),
        in_specs=[pl.BlockSpec((2, PEER_N_KEYS, 8, LANES), lambda i, h: (h, 0, i, 0))],
        out_specs=[out_spec, out_spec, out_spec],
        out_shape=[out_sds(I32), out_sds(I32), out_sds(F32)],
        scratch_shapes=[
            pltpu.VMEM((PEER_N_KEYS, 8, LANES), F32),
            pltpu.VMEM((2, PEER_TOPK, 8, LANES), F32),
            pltpu.VMEM((2, PEER_TOPK, 8, LANES), I32),
        ],
        compiler_params=_cparams(("parallel", "parallel")),
        name="peer_topk",
    )(scores4)


def _peer_kernel(xn_ref, x1_ref, a_ref, b_ref, g_ref, ut_ref, v_ref, o_ref, wt_ref, h_ref, at_ref, bt_ref, gt_ref,
                 *, tm, te, grp):
    j = pl.program_id(1)
    n_sub = te // PEER_N_KEYS
    half_keys = PEER_N_KEYS // 2

    @pl.when(j == 0)
    def _():
        o_ref[...] = x1_ref[...]
        at_ref[...] = a_ref[...].astype(F32).T
        bt_ref[...] = b_ref[...].astype(F32).T
        gt_ref[...] = g_ref[...].T
        key_iota = lax.broadcasted_iota(I32, (PEER_N_KEYS, LANES), 0).astype(F32).astype(BF16)
        one = jnp.ones((PEER_N_KEYS, LANES), BF16)
        zero = jnp.zeros((PEER_N_KEYS, LANES), BF16)

        def build(i, _):
            t0 = pl.multiple_of(i * grp, grp)
            for r in range(grp):
                a_row = at_ref[pl.ds(t0 + r, 1), :].astype(BF16)
                b_row = bt_ref[pl.ds(t0 + r, 1), :].astype(BF16)
                g_row = gt_ref[pl.ds(t0 + r, 1), :].astype(BF16)
                oa = jnp.where(key_iota == a_row, one, zero)
                gob = jnp.where(key_iota == b_row, jnp.broadcast_to(g_row, one.shape), zero)
                bits = pltpu.bitcast(_dot_nt(oa, gob), U32)
                word = (bits[half_keys:] & HI16) | lax.shift_right_logical(bits[:half_keys], jnp.uint32(16))
                wt_ref[pl.ds(pl.multiple_of((t0 + r) * W_PITCH, 8), half_keys), :] = word
            return 0

        lax.fori_loop(0, tm // grp, build, 0)

    act = jax.nn.gelu(_dot(xn_ref[...], ut_ref[...]))
    first = j * n_sub
    row0 = first % half_keys
    shift = (jnp.uint32(16) * (1 - first // half_keys).astype(U32))
    for i in range(n_sub):
        words = wt_ref[pl.ds(row0 + i, tm, stride=W_PITCH), :]
        gate = pltpu.bitcast(lax.shift_left(words, shift) & HI16, F32)
        h_ref[:, i * LANES:(i + 1) * LANES] = (act[:, i * LANES:(i + 1) * LANES] * gate).astype(BF16)
    o_ref[...] += _dot(h_ref[...], v_ref[...])


def _peer(xn, x1, ids_a, ids_b, gates, u_t, v_tab, tm=512, te=2048, grp=16):
    T = xn.shape[0]
    n_sel = PEER_HEADS * PEER_TOPK
    row = lambda w: pl.BlockSpec((tm, w), lambda i, j: (i, 0))
    sel = pl.BlockSpec((n_sel, tm), lambda i, j: (0, i))
    return pl.pallas_call(
        functools.partial(_peer_kernel, tm=tm, te=te, grp=grp),
        grid=(T // tm, PEER_N_EXPERTS // te),
        in_specs=[
            row(D_MODEL), row(D_MODEL), sel, sel, sel,
            pl.BlockSpec((D_MODEL, te), lambda i, j: (0, j)),
            pl.BlockSpec((te, D_MODEL), lambda i, j: (j, 0)),
        ],
        out_specs=row(D_MODEL),
        out_shape=jax.ShapeDtypeStruct((T, D_MODEL), F32),
        scratch_shapes=[
            pltpu.VMEM((tm * W_PITCH, LANES), U32),
            pltpu.VMEM((tm, te), BF16),
            pltpu.VMEM((tm, n_sel), F32),
            pltpu.VMEM((tm, n_sel), F32),
            pltpu.VMEM((tm, n_sel), F32),
        ],
        compiler_params=_cparams(("parallel", "arbitrary")),
        name="peer_mix",
    )(xn, x1, ids_a, ids_b, gates, u_t, v_tab)


def _final_norm_kernel(x_ref, g_ref, o_ref):
    o_ref[...] = _rmsnorm(x_ref[...], g_ref[...])


def _final_norm(x, g, tm=512):
    T = x.shape[0]
    return pl.pallas_call(
        _final_norm_kernel,
        grid=(T // tm,),
        in_specs=[pl.BlockSpec((tm, D_MODEL), lambda i: (i, 0)), pl.BlockSpec((1, D_MODEL), lambda i: (0, 0))],
        out_specs=pl.BlockSpec((tm, D_MODEL), lambda i: (i, 0)),
        out_shape=jax.ShapeDtypeStruct((T, D_MODEL), F32),
        compiler_params=_cparams(("parallel",)),
        name="final_norm",
    )(x, g)


def _prep_w_in(w):
    u, v, cq = w[:, 0:256], w[:, 256:512], w[:, 512:768]
    ckv, kr = w[:, 768:896], w[:, 896:928]
    hq, hf, hi, hg = w[:, 928:1184], w[:, 1184:1440], w[:, 1440:1696], w[:, 1696:1952]
    half = MLA_ROPE // 2
    kr_sw = jnp.concatenate([kr[:, half:], kr[:, :half]], axis=1)
    z64 = jnp.zeros((D_MODEL, MLA_NOPE), w.dtype)
    z32 = jnp.zeros((D_MODEL, LANES - MLA_NOPE - MLA_ROPE), w.dtype)
    kra = jnp.concatenate([z64, kr, z32], axis=1)
    krb = jnp.concatenate([z64, kr_sw, z32], axis=1)
    w_perm = jnp.concatenate([u, v, cq, hq, hf, hi, hg, ckv, kra, krb], axis=1).astype(BF16)
    return w_perm, hi.T.astype(BF16)


def _prep_w_uq(w):
    half = MLA_ROPE // 2
    w3 = w.reshape(MLA_Q_RANK, MLA_HEADS, MLA_NOPE + MLA_ROPE)
    nope, rope = w3[:, :, :MLA_NOPE], w3[:, :, MLA_NOPE:]
    rope_sw = jnp.concatenate([rope[:, :, half:], rope[:, :, :half]], axis=-1)
    z32 = jnp.zeros((MLA_Q_RANK, MLA_HEADS, LANES - MLA_NOPE - MLA_ROPE), w.dtype)
    z64 = jnp.zeros((MLA_Q_RANK, MLA_HEADS, MLA_NOPE), w.dtype)
    main = jnp.concatenate([nope, rope, z32], axis=-1).reshape(MLA_Q_RANK, MLA_HEADS * LANES)
    swap = jnp.concatenate([z64, rope_sw, z32], axis=-1).reshape(MLA_Q_RANK, MLA_HEADS * LANES)
    return jnp.concatenate([main, swap], axis=1).astype(BF16)


def _prep_w_ukv(w):
    w3 = w.reshape(MLA_KV_RANK, MLA_HEADS, MLA_NOPE + MLA_V)
    k_nope, v = w3[:, :, :MLA_NOPE], w3[:, :, MLA_NOPE:]
    z = jnp.zeros((MLA_KV_RANK, MLA_HEADS, LANES - MLA_NOPE), w.dtype)
    wk_pad = jnp.concatenate([k_nope, z], axis=-1).reshape(MLA_KV_RANK, MLA_HEADS * LANES).astype(BF16)
    wv_t = v.reshape(MLA_KV_RANK, MLA_WIDTH).T.astype(BF16)
    return wk_pad, wv_t


def kernel(x, positions, norm_mix, w_in, gm_v_norm, gm_ws, gm_b, gm_out_norm, mla_q_norm, mla_w_uq, mla_kv_norm, mla_w_ukv, mla_out_norm, hg_lb_logits, hg_out_norm, w_out, norm_ffn, peer_w_q, peer_sub_keys, peer_u, peer_v, norm_final):
    batch, seq, _ = x.shape
    T = batch * seq
    depth = w_in.shape[0]
    xt = x.reshape(T, D_MODEL)
    pos_b = jnp.broadcast_to(positions.reshape(T, 1).astype(F32), (T, LANES))
    cos_t, sin_t = _rope_tables(pos_b)
    p = jax.nn.softmax(hg_lb_logits.astype(F32), axis=0)
    lower = jnp.cumsum(p, axis=0) - p[0]
    msum = jnp.asarray(_hgrn_sum_matrix(), BF16)
    row = lambda v: v.reshape(1, -1)

    for l in range(depth):
        w_perm, w_hit = _prep_w_in(w_in[l])
        proj, hit = _inproj(xt, row(norm_mix[l]), w_perm, w_hit)
        bias2d = jnp.repeat(gm_b[l].T, HEAD_DIM, axis=1)
        ya = _gmlp(proj, row(gm_v_norm[l]), gm_ws[l], bias2d, row(gm_out_norm[l]))
        wk_pad, wv_t = _prep_w_ukv(mla_w_ukv[l])
        q, k, vt = _mla_proj(proj, cos_t, sin_t, row(mla_q_norm[l]), row(mla_kv_norm[l]),
                             _prep_w_uq(mla_w_uq[l]), wk_pad, wv_t)
        gain_b = jnp.broadcast_to(mla_out_norm[l][:, None], (MLA_WIDTH, LANES))
        yb = _attention(q, k, vt, gain_b, batch, seq)
        yc = _hgrn(proj, hit, row(lower[l]), row(hg_out_norm[l]), msum, batch, seq)
        wo = w_out[l].astype(BF16)
        sub_keys = peer_sub_keys[l].reshape(2 * PEER_HEADS, PEER_N_KEYS, PEER_QDIM // 2).astype(BF16)
        x1, xn, scores = _outproj(ya, yb, yc, xt, wo[:GM_WIDTH], wo[GM_WIDTH:GM_WIDTH + MLA_WIDTH],
                                  wo[GM_WIDTH + MLA_WIDTH:], row(norm_ffn[l]), peer_w_q[l].T.astype(BF16), sub_keys)
        ia, ib, gt = _topk(scores.reshape(2 * PEER_HEADS, PEER_N_KEYS, T // LANES, LANES))
        flat = lambda a: a.reshape(PEER_HEADS * PEER_TOPK, T)
        xt = _peer(xn, x1, flat(ia), flat(ib), flat(gt),
                   peer_u[l].T.astype(BF16), peer_v[l].astype(BF16))
    return _final_norm(xt, row(norm_final)).reshape(batch, seq, D_MODEL)
```
